```python
import math
import jax
import jax.numpy as jnp
from jax import lax
import numpy as np

D_MODEL = 2048
BATCH = 16
SEQ = 2048
DEPTH = 2

N_MEM = 256
XATTN_HEADS = 4
XATTN_DIM = D_MODEL // XATTN_HEADS

GDN_DIM = 128
GDN_HEADS = (3 * D_MODEL) // (8 * GDN_DIM)
GDN_WIDTH = GDN_HEADS * GDN_DIM
CONV_K = 4
GDN_CHUNK = 64

RET_DIM = 128
RET_HEADS = D_MODEL // (4 * RET_DIM)
RET_WIDTH = RET_HEADS * RET_DIM
RET_CHUNK = 64
ROPE_BASE = 10000.0

RWKV_DIM = 64
RWKV_HEADS = (3 * D_MODEL) // (8 * RWKV_DIM)
RWKV_WIDTH = RWKV_HEADS * RWKV_DIM
W_LORA = 64
A_LORA = 64
G_LORA = 128
RWKV_LNX_EPS = 64e-5

MIX_WIDTH = GDN_WIDTH + RET_WIDTH + RWKV_WIDTH
RWKV_IN = 3 * RWKV_WIDTH + W_LORA + A_LORA + G_LORA
RWKV_SPLITS = tuple(int(s) for s in np.cumsum((RWKV_WIDTH, RWKV_WIDTH, RWKV_WIDTH, W_LORA, A_LORA)))
IN_SIZES = (3 * GDN_WIDTH, GDN_WIDTH, GDN_HEADS, GDN_HEADS, 4 * RET_WIDTH, RWKV_IN)
IN_SPLITS = tuple(int(s) for s in np.cumsum(IN_SIZES)[:-1])
IN_WIDTH = int(sum(IN_SIZES))

FFN_HIDDEN = ((8 * D_MODEL + 3 * 256 - 1) // (3 * 256)) * 256

DEEPNORM_ALPHA = (2 * DEPTH) ** 0.25
DEEPNORM_BETA = (8 * DEPTH) ** -0.25

kernel_name = 'hybrid_gdn_retention_rwkv7_deepnorm_block'

F32 = jnp.float32


def layer_norm(x, g, b, eps=1e-5):
    xf = x.astype(F32)
    mu = jnp.mean(xf, -1, keepdims=True)
    var = jnp.mean(jnp.square(xf - mu), -1, keepdims=True)
    return ((xf - mu) * lax.rsqrt(var + eps) * g.astype(F32) + b.astype(F32)).astype(x.dtype)


def head_norm(y, eps):
    mu = jnp.mean(y, -1, keepdims=True)
    var = jnp.mean(jnp.square(y - mu), -1, keepdims=True)
    return (y - mu) * lax.rsqrt(var + eps)


def l2_normalize(y, eps=1e-6):
    return y * lax.rsqrt(jnp.sum(jnp.square(y), -1, keepdims=True) + eps)


def causal_conv(x, w):
    return lax.conv_general_dilated(
        x, w[:, None, :].astype(x.dtype), window_strides=(1,), padding=[(w.shape[0] - 1, 0)],
        dimension_numbers=('NWC', 'WIO', 'NWC'), feature_group_count=x.shape[-1])


def to_chunks(y, chunk):
    b, t, h, d = y.shape
    return y.reshape(b, t // chunk, chunk, h, d).transpose(1, 0, 3, 2, 4)


def from_chunks(y):
    n, b, h, c, d = y.shape
    return y.transpose(1, 0, 3, 2, 4).reshape(b, n * c, h, d)


def rotary(y, positions):
    d = y.shape[-1]
    inv_freq = ROPE_BASE ** (-jnp.arange(0, d, 2, dtype=F32) / d)
    ang = positions.astype(F32)[..., None] * inv_freq
    cos, sin = jnp.cos(ang)[:, :, None, :], jnp.sin(ang)[:, :, None, :]
    y1, y2 = jnp.split(y, 2, axis=-1)
    return jnp.concatenate([y1 * cos - y2 * sin, y2 * cos + y1 * sin], -1)


def gated_deltanet(qkv, gate, beta_logit, a_logit, conv_w, a_log, dt_bias, norm_w):
    b, t, _ = qkv.shape
    h, d, c = GDN_HEADS, GDN_DIM, GDN_CHUNK
    qkv = jax.nn.silu(causal_conv(qkv, conv_w)).astype(F32)
    q, k, v = (y.reshape(b, t, h, d) for y in jnp.split(qkv, 3, axis=-1))
    q = l2_normalize(q) * d ** -0.5
    k = l2_normalize(k)
    beta = jax.nn.sigmoid(beta_logit.astype(F32))
    log_a = -jnp.exp(a_log.astype(F32)) * jax.nn.softplus(a_logit.astype(F32) + dt_bias.astype(F32))
    qc, kc, vc = to_chunks(q, c), to_chunks(k, c), to_chunks(v, c)
    bc = to_chunks(beta[..., None], c)[..., 0]
    cum = jnp.cumsum(to_chunks(log_a[..., None], c)[..., 0], axis=-1)
    causal = jnp.tril(jnp.ones((c, c), bool))
    strict = jnp.tril(jnp.ones((c, c), bool), -1)
    diff = cum[..., :, None] - cum[..., None, :]
    seg = jnp.where(causal, jnp.exp(jnp.where(causal, diff, 0.0)), 0.0)
    kb = kc * bc[..., None]
    lhs = jnp.eye(c, dtype=F32) + jnp.where(strict, jnp.einsum('nbhid,nbhjd->nbhij', kb, kc) * seg, 0.0)
    u = lax.linalg.triangular_solve(lhs, vc * bc[..., None], left_side=True, lower=True, unit_diagonal=True)
    w = lax.linalg.triangular_solve(lhs, kb * jnp.exp(cum)[..., None], left_side=True, lower=True, unit_diagonal=True)
    qk = jnp.einsum('nbhid,nbhjd->nbhij', qc, kc) * seg

    def step(S, inp):
        q_i, k_i, u_i, w_i, cum_i, qk_i = inp
        v_new = u_i - jnp.einsum('bhck,bhkv->bhcv', w_i, S)
        o = (jnp.einsum('bhck,bhkv->bhcv', q_i * jnp.exp(cum_i)[..., None], S)
             + jnp.einsum('bhij,bhjv->bhiv', qk_i, v_new))
        last = cum_i[..., -1:]
        S = (S * jnp.exp(last)[..., None]
             + jnp.einsum('bhck,bhcv->bhkv', k_i * jnp.exp(last - cum_i)[..., None], v_new))
        return S, o

    _, o = lax.scan(step, jnp.zeros((b, h, d, d), F32), (qc, kc, u, w, cum, qk))
    o = from_chunks(o)
    o = o * lax.rsqrt(jnp.mean(jnp.square(o), -1, keepdims=True) + 1e-6) * norm_w.astype(F32)
    o = o * jax.nn.silu(gate.astype(F32)).reshape(b, t, h, d)
    return o.reshape(b, t, h * d)


def retention(qkvg, positions):
    b, t, _ = qkvg.shape
    h, d, c = RET_HEADS, RET_DIM, RET_CHUNK
    q, k, v, g = jnp.split(qkvg.astype(F32), 4, axis=-1)
    q = rotary(q.reshape(b, t, h, d), positions)
    k = rotary(k.reshape(b, t, h, d), positions) * d ** -0.5
    v = v.reshape(b, t, h, d)
    log_gamma = jnp.log1p(-jnp.exp2(-5.0 - jnp.arange(h, dtype=F32)))
    pos = jnp.arange(c, dtype=F32)
    rel = pos[:, None] - pos[None, :]
    decay_mask = jnp.where(rel >= 0, jnp.exp(jnp.where(rel >= 0, rel, 0.0) * log_gamma[:, None, None]), 0.0)
    qc, kc, vc = to_chunks(q, c), to_chunks(k, c), to_chunks(v, c)
    intra = jnp.einsum('nbhij,nbhjv->nbhiv', jnp.einsum('nbhid,nbhjd->nbhij', qc, kc) * decay_mask, vc)
    q_decay = jnp.exp((pos + 1.0) * log_gamma[:, None])
    k_decay = jnp.exp((c - 1.0 - pos) * log_gamma[:, None])
    chunk_decay = jnp.exp(c * log_gamma)

    def step(R, inp):
        q_i, k_i, v_i = inp
        o = jnp.einsum('bhck,bhkv->bhcv', q_i * q_decay[:, :, None], R)
        R = R * chunk_decay[:, None, None] + jnp.einsum('bhck,bhcv->bhkv', k_i * k_decay[:, :, None], v_i)
        return R, o

    _, inter = lax.scan(step, jnp.zeros((b, h, d, d), F32), (qc, kc, vc))
    y = head_norm(from_chunks(intra + inter), 1e-6)
    return jax.nn.silu(g) * y.reshape(b, t, h * d)


def rwkv7_time_mix(p, mu, w_up, w0, a_up, a0, g_up, k_k, k_a, r_k, lnx_w, lnx_b):
    b, t, _ = p.shape
    h, n = RWKV_HEADS, RWKV_DIM
    p = p.astype(F32)
    p_prev = jnp.pad(p, ((0, 0), (1, 0), (0, 0)))[:, :-1]
    p = p + (p_prev - p) * mu.astype(F32)
    r, k, v, wd, ad, gd = jnp.split(p, RWKV_SPLITS, axis=-1)
    w_log = -jax.nn.softplus(-(w0.astype(F32) + jnp.tanh(wd) @ w_up.astype(F32))) - 0.5
    decay = jnp.exp(-jnp.exp(w_log))
    a = jax.nn.sigmoid(a0.astype(F32) + ad @ a_up.astype(F32))
    g = jax.nn.sigmoid(gd) @ g_up.astype(F32)

    def heads(y):
        return y.reshape(b, t, h, n)

    kk = l2_normalize(heads(k * k_k.astype(F32)), 1e-12)
    k = k * (1.0 + (a - 1.0) * k_a.astype(F32))
    r, k, v, decay, a = heads(r), heads(k), heads(v), heads(decay), heads(a)

    def step(S, inp):
        r_t, w_t, k_t, v_t, a_t, b_t = inp
        sa = jnp.einsum('bhij,bhj->bhi', S, a_t)
        S = S * w_t[:, :, None, :] + sa[..., None] * b_t[:, :, None, :] + v_t[..., None] * k_t[:, :, None, :]
        return S, jnp.einsum('bhij,bhj->bhi', S, r_t)

    seq_major = [jnp.swapaxes(y, 0, 1) for y in (r, decay, k, v, -kk, kk * a)]
    _, y = lax.scan(step, jnp.zeros((b, h, n, n), F32), tuple(seq_major))
    y = jnp.swapaxes(y, 0, 1)
    y = head_norm(y, RWKV_LNX_EPS) * lnx_w.astype(F32).reshape(h, n) + lnx_b.astype(F32).reshape(h, n)
    y = y + jnp.sum(r * k * r_k.astype(F32).reshape(h, n), -1, keepdims=True) * v
    return y.reshape(b, t, h * n) * g


def hybrid_mixer(x, positions, w_in, gdn_conv, gdn_a_log, gdn_dt_bias, gdn_norm,
                 rwkv_mu, rwkv_w_up, rwkv_w0, rwkv_a_up, rwkv_a0, rwkv_g_up,
                 rwkv_k_k, rwkv_k_a, rwkv_r_k, rwkv_lnx_w, rwkv_lnx_b, w_out):
    gdn_qkv, gdn_gate, gdn_beta, gdn_alpha, ret_qkvg, rwkv_in = jnp.split(x @ w_in, IN_SPLITS, axis=-1)
    y_a = gated_deltanet(gdn_qkv, gdn_gate, gdn_beta, gdn_alpha, gdn_conv, gdn_a_log, gdn_dt_bias, gdn_norm)
    y_b = retention(ret_qkvg, positions)
    y_c = rwkv7_time_mix(rwkv_in, rwkv_mu, rwkv_w_up, rwkv_w0, rwkv_a_up, rwkv_a0, rwkv_g_up,
                         rwkv_k_k, rwkv_k_a, rwkv_r_k, rwkv_lnx_w, rwkv_lnx_b)
    y = jnp.concatenate([y_a, y_b, y_c], axis=-1).astype(x.dtype)
    return y @ w_out


def memory_cross_attention(x, mem, wq, wk, wv, wo):
    b, t, _ = x.shape
    m = mem.shape[1]
    q = (x @ wq).reshape(b, t, XATTN_HEADS, XATTN_DIM)
    k = (mem @ wk).reshape(b, m, XATTN_HEADS, XATTN_DIM)
    v = (mem @ wv).reshape(b, m, XATTN_HEADS, XATTN_DIM)
    s = jnp.einsum('bthd,bmhd->bhtm', q, k).astype(F32) * XATTN_DIM ** -0.5
    probs = jax.nn.softmax(s, axis=-1).astype(v.dtype)
    o = jnp.einsum('bhtm,bmhd->bthd', probs, v).reshape(b, t, XATTN_HEADS * XATTN_DIM)
    return o @ wo


def swiglu_ffn(x, w_gate_up, w_down):
    gate, up = jnp.split(x @ w_gate_up, 2, axis=-1)
    return (jax.nn.silu(gate) * up) @ w_down


def setup_inputs(seed: int = 0) -> dict:
    key = jax.random.key(seed)
    keys = jax.random.split(key, 32)
    L = DEPTH

    def normal(i, shape, scale):
        return scale * jax.random.normal(keys[i], shape, F32)

    def uniform(i, shape, lo, hi):
        return jax.random.uniform(keys[i], shape, F32, lo, hi)

    x = normal(0, (BATCH, SEQ, D_MODEL), 1.0)
    mem = normal(1, (BATCH, N_MEM, D_MODEL), 1.0)
    offset = jax.random.randint(keys[2], (BATCH, 1), 0, 1024, dtype=jnp.int32)
    positions = jnp.arange(SEQ, dtype=jnp.int32)[None, :] + offset
    dt = jnp.exp(uniform(6, (L, GDN_HEADS), math.log(1e-3), math.log(1e-1)))
    return {
        'x': x,
        'mem': mem,
        'positions': positions,
        'w_in': normal(3, (L, D_MODEL, IN_WIDTH), D_MODEL ** -0.5),
        'gdn_conv': normal(4, (L, CONV_K, 3 * GDN_WIDTH), CONV_K ** -0.5),
        'gdn_a_log': jnp.log(uniform(5, (L, GDN_HEADS), 1.0, 16.0)),
        'gdn_dt_bias': dt + jnp.log(-jnp.expm1(-dt)),
        'gdn_norm': 1.0 + normal(7, (L, GDN_DIM), 0.02),
        'rwkv_mu': uniform(8, (L, RWKV_IN), 0.0, 1.0),
        'rwkv_w_up': normal(9, (L, W_LORA, RWKV_WIDTH), 0.1),
        'rwkv_w0': uniform(10, (L, RWKV_WIDTH), -6.0, 1.0),
        'rwkv_a_up': normal(11, (L, A_LORA, RWKV_WIDTH), 0.5 * A_LORA ** -0.5),
        'rwkv_a0': normal(12, (L, RWKV_WIDTH), 0.1),
        'rwkv_g_up': normal(13, (L, G_LORA, RWKV_WIDTH), G_LORA ** -0.5),
        'rwkv_k_k': 0.85 + normal(14, (L, RWKV_WIDTH), 0.02),
        'rwkv_k_a': 1.0 + normal(15, (L, RWKV_WIDTH), 0.02),
        'rwkv_r_k': normal(16, (L, RWKV_WIDTH), 0.1),
        'rwkv_lnx_w': 1.0 + normal(17, (L, RWKV_WIDTH), 0.02),
        'rwkv_lnx_b': normal(18, (L, RWKV_WIDTH), 0.02),
        'w_out': normal(19, (L, MIX_WIDTH, D_MODEL), DEEPNORM_BETA * MIX_WIDTH ** -0.5),
        'ln1_g': 1.0 + normal(20, (L, D_MODEL), 0.02),
        'ln1_b': normal(21, (L, D_MODEL), 0.02),
        'xattn_q': normal(22, (L, D_MODEL, D_MODEL), D_MODEL ** -0.5),
        'xattn_k': normal(23, (L, D_MODEL, D_MODEL), D_MODEL ** -0.5),
        'xattn_v': normal(24, (L, D_MODEL, D_MODEL), D_MODEL ** -0.5),
        'xattn_o': normal(25, (L, D_MODEL, D_MODEL), DEEPNORM_BETA * D_MODEL ** -0.5),
        'ln2_g': 1.0 + normal(26, (L, D_MODEL), 0.02),
        'ln2_b': normal(27, (L, D_MODEL), 0.02),
        'ffn_gate_up': normal(28, (L, D_MODEL, 2 * FFN_HIDDEN), D_MODEL ** -0.5),
        'ffn_down': normal(29, (L, FFN_HIDDEN, D_MODEL), DEEPNORM_BETA * FFN_HIDDEN ** -0.5),
        'ln3_g': 1.0 + normal(30, (L, D_MODEL), 0.02),
        'ln3_b': normal(31, (L, D_MODEL), 0.02),
    }


def reference(x, mem, positions, w_in, gdn_conv, gdn_a_log, gdn_dt_bias, gdn_norm,
              rwkv_mu, rwkv_w_up, rwkv_w0, rwkv_a_up, rwkv_a0, rwkv_g_up,
              rwkv_k_k, rwkv_k_a, rwkv_r_k, rwkv_lnx_w, rwkv_lnx_b, w_out,
              ln1_g, ln1_b, xattn_q, xattn_k, xattn_v, xattn_o, ln2_g, ln2_b,
              ffn_gate_up, ffn_down, ln3_g, ln3_b):
    for l in range(DEPTH):
        mix = hybrid_mixer(x, positions, w_in[l], gdn_conv[l], gdn_a_log[l], gdn_dt_bias[l], gdn_norm[l],
                           rwkv_mu[l], rwkv_w_up[l], rwkv_w0[l], rwkv_a_up[l], rwkv_a0[l], rwkv_g_up[l],
                           rwkv_k_k[l], rwkv_k_a[l], rwkv_r_k[l], rwkv_lnx_w[l], rwkv_lnx_b[l], w_out[l])
        x = layer_norm(DEEPNORM_ALPHA * x + mix, ln1_g[l], ln1_b[l])
        xa = memory_cross_attention(x, mem, xattn_q[l], xattn_k[l], xattn_v[l], xattn_o[l])
        x = layer_norm(DEEPNORM_ALPHA * x + xa, ln2_g[l], ln2_b[l])
        x = layer_norm(DEEPNORM_ALPHA * x + swiglu_ffn(x, ffn_gate_up[l], ffn_down[l]), ln3_g[l], ln3_b[l])
    return x
```

```python
import functools
import math

import numpy as np
import jax
import jax.numpy as jnp
from jax import lax
from jax.experimental import pallas as pl
from jax.experimental.pallas import tpu as pltpu

F32 = jnp.float32
BF16 = jnp.bfloat16

D_MODEL = 2048
DEPTH = 2
N_MEM = 256
XATTN_HEADS = 4
XATTN_DIM = D_MODEL // XATTN_HEADS

GDN_DIM = 128
GDN_HEADS = 6
GDN_WIDTH = GDN_HEADS * GDN_DIM
CONV_K = 4

RET_DIM = 128
RET_HEADS = 4
RET_WIDTH = RET_HEADS * RET_DIM
ROPE_BASE = 10000.0

RWKV_DIM = 64
RWKV_HEADS = 12
RWKV_WIDTH = RWKV_HEADS * RWKV_DIM
RWKV_PAIRS = RWKV_HEADS // 2
W_LORA = 64
A_LORA = 64
G_LORA = 128
RWKV_LNX_EPS = 64e-5
RWKV_IN = 3 * RWKV_WIDTH + W_LORA + A_LORA + G_LORA

CHUNK = 64
LANES = 128
FFN_HIDDEN = 5632
DEEPNORM_ALPHA = (2 * DEPTH) ** 0.25

GDN_COLS = 4 * GDN_WIDTH + 2 * LANES
VMEM_LIMIT = 56 * 1024 * 1024


def _cparams(*sem):
    return pltpu.CompilerParams(dimension_semantics=sem, vmem_limit_bytes=VMEM_LIMIT)


def _bdot(a, b):
    return jnp.dot(a.astype(BF16), b.astype(BF16), preferred_element_type=F32)


def _bdot_nt(a, b):
    return lax.dot_general(a.astype(BF16), b.astype(BF16), (((1,), (1,)), ((), ())),
                           preferred_element_type=F32)


def _bdot_tn(a, b):
    return lax.dot_general(a.astype(BF16), b.astype(BF16), (((0,), (0,)), ((), ())),
                           preferred_element_type=F32)


def _split3(a):
    hi = a.astype(BF16)
    r1 = a - hi.astype(F32)
    mid = r1.astype(BF16)
    lo = (r1 - mid.astype(F32)).astype(BF16)
    return hi, mid, lo


def _dot_exact_lhs(m01, a):
    m = m01.astype(BF16)
    hi, mid, lo = _split3(a)
    return (jnp.dot(m, hi, preferred_element_type=F32) + jnp.dot(m, mid, preferred_element_type=F32)
            + jnp.dot(m, lo, preferred_element_type=F32))


def _dot_exact_rhs(a, m01):
    m = m01.astype(BF16)
    hi, mid, lo = _split3(a)
    return (jnp.dot(hi, m, preferred_element_type=F32) + jnp.dot(mid, m, preferred_element_type=F32)
            + jnp.dot(lo, m, preferred_element_type=F32))


def _dot3(a, b):
    ah = a.astype(BF16)
    al = (a - ah.astype(F32)).astype(BF16)
    bh = b.astype(BF16)
    bl = (b - bh.astype(F32)).astype(BF16)
    return (jnp.dot(ah, bh, preferred_element_type=F32) + jnp.dot(al, bh, preferred_element_type=F32)
            + jnp.dot(ah, bl, preferred_element_type=F32))


def _unit_lower_inverse(n):
    c = n.shape[0]
    rows = lax.broadcasted_iota(jnp.int32, (c, c), 0)
    cols = lax.broadcasted_iota(jnp.int32, (c, c), 1)
    x = jnp.where(rows == cols, 1.0, 0.0) + n
    p = n
    steps = int(math.log2(c)) - 1
    for s in range(steps):
        p = _dot3(p, p)
        x = x + _dot3(x, p)
    return x


def _tri_masks(c):
    rows = lax.broadcasted_iota(jnp.int32, (c, c), 0)
    cols = lax.broadcasted_iota(jnp.int32, (c, c), 1)
    return rows >= cols, rows > cols


def _sigmoid(x):
    return 1.0 / (1.0 + jnp.exp(-x))


def _silu(x):
    return x * _sigmoid(x)


def _softplus(x):
    return jnp.maximum(x, 0.0) + jnp.log1p(jnp.exp(-jnp.abs(x)))


def _mm_kernel(x_ref, w_ref, o_ref):
    o_ref[...] = jnp.dot(x_ref[...].astype(BF16), w_ref[...], preferred_element_type=F32).astype(o_ref.dtype)


def _matmul(x, w, *, tm, tn, out_dtype):
    m, k = x.shape
    n = w.shape[1]
    assert m % tm == 0 and n % tn == 0
    return pl.pallas_call(
        _mm_kernel,
        grid=(m // tm, n // tn),
        in_specs=[pl.BlockSpec((tm, k), lambda i, j: (i, 0)),
                  pl.BlockSpec((k, tn), lambda i, j: (0, j))],
        out_specs=pl.BlockSpec((tm, tn), lambda i, j: (i, j)),
        out_shape=jax.ShapeDtypeStruct((m, n), out_dtype),
        compiler_params=_cparams("parallel", "arbitrary"),
        name="proj_matmul",
    )(x, w)


def _mm_ln_kernel(x_ref, w_ref, res_ref, g_ref, b_ref, of_ref, ob_ref, acc_ref, *, nk):
    k = pl.program_id(1)

    @pl.when(k == 0)
    def _():
        acc_ref[...] = jnp.zeros_like(acc_ref)

    acc_ref[...] += jnp.dot(x_ref[...].astype(BF16), w_ref[...], preferred_element_type=F32)

    @pl.when(k == nk - 1)
    def _():
        y = DEEPNORM_ALPHA * res_ref[...] + acc_ref[...]
        mu = jnp.mean(y, axis=-1, keepdims=True)
        yc = y - mu
        var = jnp.mean(yc * yc, axis=-1, keepdims=True)
        out = yc * lax.rsqrt(var + 1e-5) * g_ref[...] + b_ref[...]
        of_ref[...] = out
        ob_ref[...] = out.astype(BF16)


def _matmul_residual_ln(x, w, res, g, b, *, tm, tk):
    m, kdim = x.shape
    n = w.shape[1]
    assert m % tm == 0 and kdim % tk == 0
    nk = kdim // tk
    return pl.pallas_call(
        functools.partial(_mm_ln_kernel, nk=nk),
        grid=(m // tm, nk),
        in_specs=[pl.BlockSpec((tm, tk), lambda i, k: (i, k)),
                  pl.BlockSpec((tk, n), lambda i, k: (k, 0)),
                  pl.BlockSpec((tm, n), lambda i, k: (i, 0)),
                  pl.BlockSpec((1, n), lambda i, k: (0, 0)),
                  pl.BlockSpec((1, n), lambda i, k: (0, 0))],
        out_specs=[pl.BlockSpec((tm, n), lambda i, k: (i, 0)),
                   pl.BlockSpec((tm, n), lambda i, k: (i, 0))],
        out_shape=[jax.ShapeDtypeStruct((m, n), F32), jax.ShapeDtypeStruct((m, n), BF16)],
        scratch_shapes=[pltpu.VMEM((tm, n), F32)],
        compiler_params=_cparams("parallel", "arbitrary"),
        name="proj_residual_layernorm",
    )(x, w, res, g.reshape(1, n), b.reshape(1, n))


def _swiglu_kernel(x_ref, wg_ref, wu_ref, o_ref):
    x = x_ref[...]
    gate = jnp.dot(x, wg_ref[...], preferred_element_type=F32)
    up = jnp.dot(x, wu_ref[...], preferred_element_type=F32)
    o_ref[...] = (_silu(gate) * up).astype(o_ref.dtype)


def _swiglu_up(x, w_gate_up, *, tm, tn):
    m, k = x.shape
    h = w_gate_up.shape[1] // 2
    assert m % tm == 0 and h % tn == 0
    nj = h // tn
    return pl.pallas_call(
        _swiglu_kernel,
        grid=(m // tm, nj),
        in_specs=[pl.BlockSpec((tm, k), lambda i, j: (i, 0)),
                  pl.BlockSpec((k, tn), lambda i, j: (0, j)),
                  pl.BlockSpec((k, tn), lambda i, j: (0, j + nj))],
        out_specs=pl.BlockSpec((tm, tn), lambda i, j: (i, j)),
        out_shape=jax.ShapeDtypeStruct((m, h), BF16),
        compiler_params=_cparams("parallel", "arbitrary"),
        name="swiglu_up",
    )(x, w_gate_up, w_gate_up)


def _xattn_kernel(q_ref, k_ref, v_ref, o_ref):
    scale = XATTN_DIM ** -0.5
    for h in range(XATTN_HEADS):
        sl = slice(h * XATTN_DIM, (h + 1) * XATTN_DIM)
        s = lax.dot_general(q_ref[:, sl], k_ref[:, sl], (((1,), (1,)), ((), ())),
                            preferred_element_type=F32) * scale
        s = s - jnp.max(s, axis=-1, keepdims=True)
        p = jnp.exp(s)
        p = p / jnp.sum(p, axis=-1, keepdims=True)
        o_ref[:, sl] = jnp.dot(p.astype(BF16), v_ref[:, sl], preferred_element_type=F32).astype(o_ref.dtype)


def _xattn_core(q, k, v, *, batch, tq):
    m = q.shape[0]
    t = m // batch
    assert t % tq == 0
    nt = t // tq
    return pl.pallas_call(
        _xattn_kernel,
        grid=(batch, nt),
        in_specs=[pl.BlockSpec((tq, D_MODEL), lambda b, i: (b * nt + i, 0)),
                  pl.BlockSpec((N_MEM, D_MODEL), lambda b, i: (b, 0)),
                  pl.BlockSpec((N_MEM, D_MODEL), lambda b, i: (b, 0))],
        out_specs=pl.BlockSpec((tq, D_MODEL), lambda b, i: (b * nt + i, 0)),
        out_shape=jax.ShapeDtypeStruct((m, D_MODEL), BF16),
        compiler_params=_cparams("parallel", "arbitrary"),
        name="xattn_core",
    )(q, k, v)


def _retention_kernel(p_ref, pos_ref, freq_ref, sign_ref, o_ref, state_ref):
    c = CHUNK

    @pl.when(pl.program_id(1) == 0)
    def _():
        state_ref[...] = jnp.zeros_like(state_ref)

    ang = pos_ref[...] * freq_ref[...]
    cos = jnp.cos(ang)
    sin_signed = jnp.sin(ang) * sign_ref[...]

    def rotary(y):
        return y * cos + pltpu.roll(y, RET_DIM // 2, 1) * sin_signed

    rows = lax.broadcasted_iota(jnp.int32, (c, c), 0)
    cols = lax.broadcasted_iota(jnp.int32, (c, c), 1)
    rel = (rows - cols).astype(F32)
    tpos = lax.broadcasted_iota(jnp.int32, (c, RET_DIM), 0).astype(F32)

    for h in range(RET_HEADS):
        log_gamma = math.log1p(-2.0 ** (-5.0 - h))
        sl = slice(h * RET_DIM, (h + 1) * RET_DIM)
        q = rotary(p_ref[:, sl])
        k = rotary(p_ref[:, RET_WIDTH + h * RET_DIM:RET_WIDTH + (h + 1) * RET_DIM]) * RET_DIM ** -0.5
        v = p_ref[:, 2 * RET_WIDTH + h * RET_DIM:2 * RET_WIDTH + (h + 1) * RET_DIM]
        g = p_ref[:, 3 * RET_WIDTH + h * RET_DIM:3 * RET_WIDTH + (h + 1) * RET_DIM]
        decay_mask = jnp.where(rel >= 0, jnp.exp(jnp.where(rel >= 0, rel, 0.0) * log_gamma), 0.0)
        q_decay = jnp.exp((tpos + 1.0) * log_gamma)
        k_decay = jnp.exp((c - 1.0 - tpos) * log_gamma)
        chunk_decay = math.exp(c * log_gamma)
        state = state_ref[h]
        intra = _bdot(_bdot_nt(q, k) * decay_mask, v)
        inter = _bdot(q * q_decay, state)
        state_ref[h] = state * chunk_decay + _bdot_tn(k * k_decay, v)
        y = intra + inter
        mu = jnp.mean(y, axis=-1, keepdims=True)
        yc = y - mu
        var = jnp.mean(yc * yc, axis=-1, keepdims=True)
        o_ref[:, sl] = (_silu(g) * (yc * lax.rsqrt(var + 1e-6))).astype(o_ref.dtype)


def _retention(proj, pos, *, batch):
    m = proj.shape[0]
    nt = m // batch // CHUNK
    d = RET_DIM
    inv_freq = ROPE_BASE ** (-jnp.arange(0, d, 2, dtype=F32) / d)
    freq2 = jnp.concatenate([inv_freq, inv_freq]).reshape(1, d)
    sign = jnp.concatenate([-jnp.ones((d // 2,), F32), jnp.ones((d // 2,), F32)]).reshape(1, d)
    return pl.pallas_call(
        _retention_kernel,
        grid=(batch, nt),
        in_specs=[pl.BlockSpec((CHUNK, 4 * RET_WIDTH), lambda b, i: (b * nt + i, 0)),
                  pl.BlockSpec((CHUNK, 1), lambda b, i: (b * nt + i, 0)),
                  pl.BlockSpec((1, d), lambda b, i: (0, 0)),
                  pl.BlockSpec((1, d), lambda b, i: (0, 0))],
        out_specs=pl.BlockSpec((CHUNK, RET_WIDTH), lambda b, i: (b * nt + i, 0)),
        out_shape=jax.ShapeDtypeStruct((m, RET_WIDTH), BF16),
        scratch_shapes=[pltpu.VMEM((RET_HEADS, d, d), F32)],
        compiler_params=_cparams("parallel", "arbitrary"),
        name="retention",
    )(proj, pos, freq2, sign)


def _gdn_kernel(p_ref, conv_ref, alog_ref, dtb_ref, nw_ref, o_ref, xpad_ref, state_ref):
    c = CHUNK
    w3 = 3 * GDN_WIDTH
    halo = 8

    @pl.when(pl.program_id(1) == 0)
    def _():
        state_ref[...] = jnp.zeros_like(state_ref)
        xpad_ref[0:halo, :] = jnp.zeros((halo, w3), F32)

    @pl.when(pl.program_id(1) != 0)
    def _():
        xpad_ref[0:halo, :] = xpad_ref[c:c + halo, :]

    xpad_ref[halo:halo + c, :] = p_ref[:, 0:w3]
    conv = conv_ref[0:1, :] * xpad_ref[halo - 3:halo - 3 + c, :]
    for j in range(1, CONV_K):
        conv = conv + conv_ref[j:j + 1, :] * xpad_ref[halo - 3 + j:halo - 3 + j + c, :]
    qkv = _silu(conv)

    beta_all = _sigmoid(p_ref[:, 4 * GDN_WIDTH:4 * GDN_WIDTH + LANES])
    log_a = -jnp.exp(alog_ref[...]) * _softplus(p_ref[:, 4 * GDN_WIDTH + LANES:GDN_COLS] + dtb_ref[...])
    causal, strict = _tri_masks(c)
    tril01 = jnp.where(causal, 1.0, 0.0)
    cum_all = _dot_exact_lhs(tril01, log_a)
    cum_rows = cum_all.T

    for h in range(GDN_HEADS):
        sl = slice(h * GDN_DIM, (h + 1) * GDN_DIM)
        q = qkv[:, sl]
        k = qkv[:, GDN_WIDTH + h * GDN_DIM:GDN_WIDTH + (h + 1) * GDN_DIM]
        v = qkv[:, 2 * GDN_WIDTH + h * GDN_DIM:2 * GDN_WIDTH + (h + 1) * GDN_DIM]
        q = q * lax.rsqrt(jnp.sum(q * q, axis=-1, keepdims=True) + 1e-6) * GDN_DIM ** -0.5
        k = k * lax.rsqrt(jnp.sum(k * k, axis=-1, keepdims=True) + 1e-6)
        beta = beta_all[:, h:h + 1]
        cum = cum_all[:, h:h + 1]
        diff = cum - cum_rows[h:h + 1, :]
        seg = jnp.where(causal, jnp.exp(jnp.where(causal, diff, 0.0)), 0.0)
        kb = k * beta
        a_mat = jnp.where(strict, _bdot_nt(kb, k) * seg, 0.0)
        t_inv = _unit_lower_inverse(-a_mat)
        ecum = jnp.exp(cum)
        uw = _dot3(t_inv, jnp.concatenate([v * beta, kb * ecum], axis=1))
        u = uw[:, 0:GDN_DIM]
        w = uw[:, GDN_DIM:2 * GDN_DIM]
        qk = _bdot_nt(q, k) * seg
        state = state_ref[h]
        v_new = u - _bdot(w, state)
        o = _bdot(q * ecum, state) + _bdot(qk, v_new)
        last = cum_all[c - 1:c, h:h + 1]
        state_ref[h] = state * jnp.exp(last) + _bdot_tn(k * jnp.exp(last - cum), v_new)
        o = o * lax.rsqrt(jnp.mean(o * o, axis=-1, keepdims=True) + 1e-6) * nw_ref[...]
        gate = p_ref[:, 3 * GDN_WIDTH + h * GDN_DIM:3 * GDN_WIDTH + (h + 1) * GDN_DIM]
        o_ref[:, sl] = (o * _silu(gate)).astype(o_ref.dtype)


def _gated_deltanet(proj, conv_w, a_log, dt_bias, norm_w, *, batch):
    m = proj.shape[0]
    nt = m // batch // CHUNK
    pad = lambda a: jnp.pad(a.reshape(1, -1), ((0, 0), (0, LANES - a.shape[-1])))
    return pl.pallas_call(
        _gdn_kernel,
        grid=(batch, nt),
        in_specs=[pl.BlockSpec((CHUNK, GDN_COLS), lambda b, i: (b * nt + i, 0)),
                  pl.BlockSpec((CONV_K, 3 * GDN_WIDTH), lambda b, i: (0, 0)),
                  pl.BlockSpec((1, LANES), lambda b, i: (0, 0)),
                  pl.BlockSpec((1, LANES), lambda b, i: (0, 0)),
                  pl.BlockSpec((1, GDN_DIM), lambda b, i: (0, 0))],
        out_specs=pl.BlockSpec((CHUNK, GDN_WIDTH), lambda b, i: (b * nt + i, 0)),
        out_shape=jax.ShapeDtypeStruct((m, GDN_WIDTH), BF16),
        scratch_shapes=[pltpu.VMEM((CHUNK + 8, 3 * GDN_WIDTH), F32),
                        pltpu.VMEM((GDN_HEADS, GDN_DIM, GDN_DIM), F32)],
        compiler_params=_cparams("parallel", "arbitrary"),
        name="gated_deltanet",
    )(proj, conv_w, pad(a_log), pad(dt_bias), norm_w.reshape(1, GDN_DIM))


def _rwkv_kernel(p_ref, mu_ref, wup_ref, w0_ref, aup_ref, a0_ref, gup_ref, kk_ref, ka_ref, rk_ref,
                 lnw_ref, lnb_ref, o_ref, prev_ref, state_ref):
    c = CHUNK
    wd_ = RWKV_WIDTH
    halo = 8

    @pl.when(pl.program_id(1) == 0)
    def _():
        state_ref[...] = jnp.zeros_like(state_ref)
        prev_ref[0:halo, :] = jnp.zeros((halo, RWKV_IN), F32)

    @pl.when(pl.program_id(1) != 0)
    def _():
        prev_ref[0:halo, :] = prev_ref[c:c + halo, :]

    prev_ref[halo:halo + c, :] = p_ref[...]
    p = p_ref[...]
    p = p + (prev_ref[halo - 1:halo - 1 + c, :] - p) * mu_ref[...]

    r_all = p[:, 0:wd_]
    k_all = p[:, wd_:2 * wd_]
    v_all = p[:, 2 * wd_:3 * wd_]
    w_down = p[:, 3 * wd_:3 * wd_ + W_LORA]
    a_down = p[:, 3 * wd_ + W_LORA:3 * wd_ + W_LORA + A_LORA]
    g_down = p[:, 3 * wd_ + W_LORA + A_LORA:RWKV_IN]

    w_log = -_softplus(-(w0_ref[...] + _bdot(jnp.tanh(w_down), wup_ref[...]))) - 0.5
    log_w = -jnp.exp(w_log)
    a_sig = _sigmoid(a0_ref[...] + _bdot(a_down, aup_ref[...]))
    g_all = _bdot(_sigmoid(g_down), gup_ref[...])

    causal, strict = _tri_masks(c)
    tril01 = jnp.where(causal, 1.0, 0.0)
    cum_all = _dot_exact_lhs(tril01, log_w)
    lane = lax.broadcasted_iota(jnp.int32, (c, LANES), 1)
    first = lane < RWKV_DIM
    rl = lax.broadcasted_iota(jnp.int32, (LANES, LANES), 0) < RWKV_DIM
    cl = lax.broadcasted_iota(jnp.int32, (LANES, LANES), 1) < RWKV_DIM
    same_head = rl == cl
    head_ones = jnp.where(same_head, 1.0, 0.0)

    def head_sum(y):
        return _dot_exact_rhs(y, head_ones)

    for pr in range(RWKV_PAIRS):
        sl = slice(pr * LANES, (pr + 1) * LANES)
        r = r_all[:, sl]
        k0 = k_all[:, sl]
        v = v_all[:, sl]
        a = a_sig[:, sl]
        lw = log_w[:, sl]
        cum = cum_all[:, sl]
        kk = k0 * kk_ref[:, sl]
        kk = kk * lax.rsqrt(head_sum(kk * kk) + 1e-12)
        k = k0 * (1.0 + (a - 1.0) * ka_ref[:, sl])
        av = -kk
        bv = kk * a

        ref_c = cum[c // 2 - 1:c // 2, :]
        e_in = jnp.exp(cum - ref_c)
        e_out = jnp.exp(ref_c - cum)
        e_cum = jnp.exp(cum)
        e_prev = jnp.exp(cum - lw)
        last = cum[c - 1:c, :]
        e_tail = jnp.exp(last - cum)
        a_in = av * e_prev * jnp.exp(-ref_c)
        r_in = r * e_in
        b_out = bv * e_out
        k_out = k * e_out

        def scores(x_in, y_out):
            s0 = _bdot_nt(jnp.where(first, x_in, 0.0), y_out)
            s1 = _bdot_nt(jnp.where(first, 0.0, x_in), y_out)
            return s0, s1

        ab0, ab1 = scores(a_in, b_out)
        ak0, ak1 = scores(a_in, k_out)
        rb0, rb1 = scores(r_in, b_out)
        rk0, rk1 = scores(r_in, k_out)

        def per_head(m0, m1, x):
            return jnp.where(first, _bdot(m0, x), _bdot(m1, x))

        t0 = _unit_lower_inverse(jnp.where(strict, ab0, 0.0))
        t1 = _unit_lower_inverse(jnp.where(strict, ab1, 0.0))
        akv = per_head(jnp.where(strict, ak0, 0.0), jnp.where(strict, ak1, 0.0), v)
        state = state_ref[pr]
        rhs = _bdot_nt(av * e_prev, state) + akv
        u = jnp.where(first, _dot3(t0, rhs), _dot3(t1, rhs))
        y = (_bdot_nt(r * e_cum, state)
             + per_head(jnp.where(causal, rb0, 0.0), jnp.where(causal, rb1, 0.0), u)
             + per_head(jnp.where(causal, rk0, 0.0), jnp.where(causal, rk1, 0.0), v))
        upd = _bdot_tn(v, k * e_tail) + _bdot_tn(u, bv * e_tail)
        state_ref[pr] = state * jnp.exp(last) + jnp.where(same_head, upd, 0.0)

        mu = head_sum(y) * (1.0 / RWKV_DIM)
        yc = y - mu
        var = head_sum(yc * yc) * (1.0 / RWKV_DIM)
        yn = yc * lax.rsqrt(var + RWKV_LNX_EPS) * lnw_ref[:, sl] + lnb_ref[:, sl]
        yn = yn + head_sum(r * k * rk_ref[:, sl]) * v
        o_ref[:, sl] = (yn * g_all[:, sl]).astype(o_ref.dtype)


def _rwkv7(proj, mu, w_up, w0, a_up, a0, g_up, k_k, k_a, r_k, lnx_w, lnx_b, *, batch):
    m = proj.shape[0]
    nt = m // batch // CHUNK
    row = lambda a: a.reshape(1, -1)
    full = lambda a: pl.BlockSpec(a.shape, lambda b, i: (0, 0))
    params = [row(mu), w_up.astype(BF16), row(w0), a_up.astype(BF16), row(a0), g_up.astype(BF16),
              row(k_k), row(k_a), row(r_k), row(lnx_w), row(lnx_b)]
    return pl.pallas_call(
        _rwkv_kernel,
        grid=(batch, nt),
        in_specs=[pl.BlockSpec((CHUNK, RWKV_IN), lambda b, i: (b * nt + i, 0))] + [full(a) for a in params],
        out_specs=pl.BlockSpec((CHUNK, RWKV_WIDTH), lambda b, i: (b * nt + i, 0)),
        out_shape=jax.ShapeDtypeStruct((m, RWKV_WIDTH), BF16),
        scratch_shapes=[pltpu.VMEM((CHUNK + 8, RWKV_IN), F32),
                        pltpu.VMEM((RWKV_PAIRS, LANES, LANES), F32)],
        compiler_params=_cparams("parallel", "arbitrary"),
        name="rwkv7",
    )(proj, *params)


def _split_w_in(w_in):
    g3 = 3 * GDN_WIDTH
    o_gate = g3
    o_beta = o_gate + GDN_WIDTH
    o_alpha = o_beta + GDN_HEADS
    o_ret = o_alpha + GDN_HEADS
    o_rwkv = o_ret + 4 * RET_WIDTH
    zpad = jnp.zeros((w_in.shape[0], LANES - GDN_HEADS), w_in.dtype)
    w_gdn = jnp.concatenate([w_in[:, :o_beta], w_in[:, o_beta:o_alpha], zpad,
                             w_in[:, o_alpha:o_ret], zpad], axis=1)
    return (w_gdn.astype(BF16), w_in[:, o_ret:o_rwkv].astype(BF16), w_in[:, o_rwkv:].astype(BF16))


def kernel(x, mem, positions, w_in, gdn_conv, gdn_a_log, gdn_dt_bias, gdn_norm, rwkv_mu, rwkv_w_up, rwkv_w0, rwkv_a_up, rwkv_a0, rwkv_g_up, rwkv_k_k, rwkv_k_a, rwkv_r_k, rwkv_lnx_w, rwkv_lnx_b, w_out, ln1_g, ln1_b, xattn_q, xattn_k, xattn_v, xattn_o, ln2_g, ln2_b, ffn_gate_up, ffn_down, ln3_g, ln3_b):
    batch, seq, d = x.shape
    m = batch * seq
    tm = min(512, m)
    xf = x.reshape(m, d)
    xb = xf.astype(BF16)
    memb = mem.reshape(batch * N_MEM, d).astype(BF16)
    pos = positions.astype(F32).reshape(m, 1)

    for l in range(DEPTH):
        w_gdn, w_ret, w_rwkv = _split_w_in(w_in[l])
        p_gdn = _matmul(xb, w_gdn, tm=tm, tn=GDN_COLS // 2, out_dtype=F32)
        p_ret = _matmul(xb, w_ret, tm=tm, tn=4 * RET_WIDTH // 2, out_dtype=F32)
        p_rwkv = _matmul(xb, w_rwkv, tm=tm, tn=RWKV_IN // 2, out_dtype=F32)
        y_a = _gated_deltanet(p_gdn, gdn_conv[l], gdn_a_log[l], gdn_dt_bias[l], gdn_norm[l], batch=batch)
        y_b = _retention(p_ret, pos, batch=batch)
        y_c = _rwkv7(p_rwkv, rwkv_mu[l], rwkv_w_up[l], rwkv_w0[l], rwkv_a_up[l], rwkv_a0[l], rwkv_g_up[l],
                     rwkv_k_k[l], rwkv_k_a[l], rwkv_r_k[l], rwkv_lnx_w[l], rwkv_lnx_b[l], batch=batch)
        y = jnp.concatenate([y_a, y_b, y_c], axis=-1)
        xf, xb = _matmul_residual_ln(y, w_out[l].astype(BF16), xf, ln1_g[l], ln1_b[l], tm=tm, tk=d)

        q = _matmul(xb, xattn_q[l].astype(BF16), tm=tm, tn=d // 2, out_dtype=BF16)
        kmem = _matmul(memb, xattn_k[l].astype(BF16), tm=tm, tn=d // 2, out_dtype=BF16)
        vmem = _matmul(memb, xattn_v[l].astype(BF16), tm=tm, tn=d // 2, out_dtype=BF16)
        att = _xattn_core(q, kmem, vmem, batch=batch, tq=min(512, seq))
        xf, xb = _matmul_residual_ln(att, xattn_o[l].astype(BF16), xf, ln2_g[l], ln2_b[l], tm=tm, tk=d)

        hidden = _swiglu_up(xb, ffn_gate_up[l].astype(BF16), tm=tm, tn=512)
        xf, xb = _matmul_residual_ln(hidden, ffn_down[l].astype(BF16), xf, ln3_g[l], ln3_b[l], tm=tm, tk=512)

    return xf.reshape(batch, seq, d)
```

```python
import functools
import math

import numpy as np
import jax
import jax.numpy as jnp
from jax import lax
from jax.experimental import pallas as pl
from jax.experimental.pallas import tpu as pltpu

F32 = jnp.float32
BF16 = jnp.bfloat16

D_MODEL = 2048
DEPTH = 2
N_MEM = 256
XATTN_HEADS = 4
XATTN_DIM = D_MODEL // XATTN_HEADS

GDN_DIM = 128
GDN_HEADS = 6
GDN_WIDTH = GDN_HEADS * GDN_DIM
CONV_K = 4

RET_DIM = 128
RET_HEADS = 4
RET_WIDTH = RET_HEADS * RET_DIM
ROPE_BASE = 10000.0

RWKV_DIM = 64
RWKV_HEADS = 12
RWKV_WIDTH = RWKV_HEADS * RWKV_DIM
RWKV_PAIRS = RWKV_HEADS // 2
W_LORA = 64
A_LORA = 64
G_LORA = 128
RWKV_LNX_EPS = 64e-5
RWKV_IN = 3 * RWKV_WIDTH + W_LORA + A_LORA + G_LORA

CHUNK = 64
LANES = 128
FFN_HIDDEN = 5632
DEEPNORM_ALPHA = (2 * DEPTH) ** 0.25

GDN_COLS = 4 * GDN_WIDTH + 2 * LANES
VMEM_LIMIT = 56 * 1024 * 1024


def _cparams(*sem):
    return pltpu.CompilerParams(dimension_semantics=sem, vmem_limit_bytes=VMEM_LIMIT)


def _bdot(a, b):
    return jnp.dot(a.astype(BF16), b.astype(BF16), preferred_element_type=F32)


def _bdot_nt(a, b):
    return lax.dot_general(a.astype(BF16), b.astype(BF16), (((1,), (1,)), ((), ())),
                           preferred_element_type=F32)


def _bdot_tn(a, b):
    return lax.dot_general(a.astype(BF16), b.astype(BF16), (((0,), (0,)), ((), ())),
                           preferred_element_type=F32)


def _split3(a):
    hi = a.astype(BF16)
    r1 = a - hi.astype(F32)
    mid = r1.astype(BF16)
    lo = (r1 - mid.astype(F32)).astype(BF16)
    return hi, mid, lo


def _dot_exact_lhs(m01, a):
    m = m01.astype(BF16)
    hi, mid, lo = _split3(a)
    return (jnp.dot(m, hi, preferred_element_type=F32) + jnp.dot(m, mid, preferred_element_type=F32)
            + jnp.dot(m, lo, preferred_element_type=F32))


def _dot_split_lhs(a, m01):
    hi = a.astype(BF16)
    lo = (a - hi.astype(F32)).astype(BF16)
    return jnp.dot(hi, m01, preferred_element_type=F32) + jnp.dot(lo, m01, preferred_element_type=F32)


def _unit_lower_inverses(ns):
    c = ns[0].shape[0]
    rows = lax.broadcasted_iota(jnp.int32, (c, c), 0)
    cols = lax.broadcasted_iota(jnp.int32, (c, c), 1)
    eye = jnp.where(rows == cols, 1.0, 0.0)
    xs = [eye + n for n in ns]
    ps = list(ns)
    for _ in range(int(math.log2(c)) - 1):
        ps = [_bdot(p, p) for p in ps]
        xs = [x + _bdot(x, p) for x, p in zip(xs, ps)]
    return xs


def _tri_masks(c):
    rows = lax.broadcasted_iota(jnp.int32, (c, c), 0)
    cols = lax.broadcasted_iota(jnp.int32, (c, c), 1)
    return rows >= cols, rows > cols


def _sigmoid(x):
    return 1.0 / (1.0 + jnp.exp(-x))


def _silu(x):
    return x * _sigmoid(x)


def _softplus(x):
    return jnp.maximum(x, 0.0) + jnp.log1p(jnp.exp(-jnp.abs(x)))


def _mm_kernel(x_ref, w_ref, o_ref):
    o_ref[...] = jnp.dot(x_ref[...].astype(BF16), w_ref[...], preferred_element_type=F32).astype(o_ref.dtype)


def _matmul(x, w, *, tm, tn, out_dtype):
    m, k = x.shape
    n = w.shape[1]
    assert m % tm == 0 and n % tn == 0
    return pl.pallas_call(
        _mm_kernel,
        grid=(m // tm, n // tn),
        in_specs=[pl.BlockSpec((tm, k), lambda i, j: (i, 0)),
                  pl.BlockSpec((k, tn), lambda i, j: (0, j))],
        out_specs=pl.BlockSpec((tm, tn), lambda i, j: (i, j)),
        out_shape=jax.ShapeDtypeStruct((m, n), out_dtype),
        compiler_params=_cparams("parallel", "arbitrary"),
        name="proj_matmul",
    )(x, w)


def _mm_ln_kernel(x_ref, w_ref, res_ref, g_ref, b_ref, of_ref, ob_ref, acc_ref, *, nk):
    k = pl.program_id(1)

    @pl.when(k == 0)
    def _():
        acc_ref[...] = jnp.zeros_like(acc_ref)

    acc_ref[...] += jnp.dot(x_ref[...].astype(BF16), w_ref[...], preferred_element_type=F32)

    @pl.when(k == nk - 1)
    def _():
        y = DEEPNORM_ALPHA * res_ref[...] + acc_ref[...]
        mu = jnp.mean(y, axis=-1, keepdims=True)
        yc = y - mu
        var = jnp.mean(yc * yc, axis=-1, keepdims=True)
        out = yc * lax.rsqrt(var + 1e-5) * g_ref[...] + b_ref[...]
        of_ref[...] = out
        ob_ref[...] = out.astype(BF16)


def _matmul_residual_ln(x, w, res, g, b, *, tm, tk):
    m, kdim = x.shape
    n = w.shape[1]
    assert m % tm == 0 and kdim % tk == 0
    nk = kdim // tk
    return pl.pallas_call(
        functools.partial(_mm_ln_kernel, nk=nk),
        grid=(m // tm, nk),
        in_specs=[pl.BlockSpec((tm, tk), lambda i, k: (i, k)),
                  pl.BlockSpec((tk, n), lambda i, k: (k, 0)),
                  pl.BlockSpec((tm, n), lambda i, k: (i, 0)),
                  pl.BlockSpec((1, n), lambda i, k: (0, 0)),
                  pl.BlockSpec((1, n), lambda i, k: (0, 0))],
        out_specs=[pl.BlockSpec((tm, n), lambda i, k: (i, 0)),
                   pl.BlockSpec((tm, n), lambda i, k: (i, 0))],
        out_shape=[jax.ShapeDtypeStruct((m, n), F32), jax.ShapeDtypeStruct((m, n), BF16)],
        scratch_shapes=[pltpu.VMEM((tm, n), F32)],
        compiler_params=_cparams("parallel", "arbitrary"),
        name="proj_residual_layernorm",
    )(x, w, res, g.reshape(1, n), b.reshape(1, n))


def _swiglu_kernel(x_ref, wg_ref, wu_ref, o_ref):
    x = x_ref[...]
    gate = jnp.dot(x, wg_ref[...], preferred_element_type=F32)
    up = jnp.dot(x, wu_ref[...], preferred_element_type=F32)
    o_ref[...] = (_silu(gate) * up).astype(o_ref.dtype)


def _swiglu_up(x, w_gate_up, *, tm, tn):
    m, k = x.shape
    h = w_gate_up.shape[1] // 2
    assert m % tm == 0 and h % tn == 0
    nj = h // tn
    return pl.pallas_call(
        _swiglu_kernel,
        grid=(m // tm, nj),
        in_specs=[pl.BlockSpec((tm, k), lambda i, j: (i, 0)),
                  pl.BlockSpec((k, tn), lambda i, j: (0, j)),
                  pl.BlockSpec((k, tn), lambda i, j: (0, j + nj))],
        out_specs=pl.BlockSpec((tm, tn), lambda i, j: (i, j)),
        out_shape=jax.ShapeDtypeStruct((m, h), BF16),
        compiler_params=_cparams("parallel", "arbitrary"),
        name="swiglu_up",
    )(x, w_gate_up, w_gate_up)


def _xattn_kernel(q_ref, k_ref, v_ref, o_ref):
    scale = XATTN_DIM ** -0.5
    for h in range(XATTN_HEADS):
        sl = slice(h * XATTN_DIM, (h + 1) * XATTN_DIM)
        s = lax.dot_general(q_ref[:, sl], k_ref[:, sl], (((1,), (1,)), ((), ())),
                            preferred_element_type=F32) * scale
        s = s - jnp.max(s, axis=-1, keepdims=True)
        p = jnp.exp(s)
        p = p / jnp.sum(p, axis=-1, keepdims=True)
        o_ref[:, sl] = jnp.dot(p.astype(BF16), v_ref[:, sl], preferred_element_type=F32).astype(o_ref.dtype)


def _xattn_core(q, k, v, *, batch, tq):
    m = q.shape[0]
    t = m // batch
    assert t % tq == 0
    nt = t // tq
    return pl.pallas_call(
        _xattn_kernel,
        grid=(batch, nt),
        in_specs=[pl.BlockSpec((tq, D_MODEL), lambda b, i: (b * nt + i, 0)),
                  pl.BlockSpec((N_MEM, D_MODEL), lambda b, i: (b, 0)),
                  pl.BlockSpec((N_MEM, D_MODEL), lambda b, i: (b, 0))],
        out_specs=pl.BlockSpec((tq, D_MODEL), lambda b, i: (b * nt + i, 0)),
        out_shape=jax.ShapeDtypeStruct((m, D_MODEL), BF16),
        compiler_params=_cparams("parallel", "arbitrary"),
        name="xattn_core",
    )(q, k, v)


def _retention_kernel(p_ref, pos_ref, freq_ref, sign_ref, o_ref, state_ref):
    c = CHUNK

    @pl.when(pl.program_id(1) == 0)
    def _():
        state_ref[...] = jnp.zeros_like(state_ref)

    ang = pos_ref[...] * freq_ref[...]
    cos = jnp.cos(ang)
    sin_signed = jnp.sin(ang) * sign_ref[...]

    def rotary(y):
        return y * cos + pltpu.roll(y, RET_DIM // 2, 1) * sin_signed

    rows = lax.broadcasted_iota(jnp.int32, (c, c), 0)
    cols = lax.broadcasted_iota(jnp.int32, (c, c), 1)
    rel = (rows - cols).astype(F32)
    tpos = lax.broadcasted_iota(jnp.int32, (c, RET_DIM), 0).astype(F32)

    for h in range(RET_HEADS):
        log_gamma = math.log1p(-2.0 ** (-5.0 - h))
        sl = slice(h * RET_DIM, (h + 1) * RET_DIM)
        q = rotary(p_ref[:, sl])
        k = rotary(p_ref[:, RET_WIDTH + h * RET_DIM:RET_WIDTH + (h + 1) * RET_DIM]) * RET_DIM ** -0.5
        v = p_ref[:, 2 * RET_WIDTH + h * RET_DIM:2 * RET_WIDTH + (h + 1) * RET_DIM]
        g = p_ref[:, 3 * RET_WIDTH + h * RET_DIM:3 * RET_WIDTH + (h + 1) * RET_DIM]
        decay_mask = jnp.where(rel >= 0, jnp.exp(jnp.where(rel >= 0, rel, 0.0) * log_gamma), 0.0)
        q_decay = jnp.exp((tpos + 1.0) * log_gamma)
        k_decay = jnp.exp((c - 1.0 - tpos) * log_gamma)
        chunk_decay = math.exp(c * log_gamma)
        state = state_ref[h]
        intra = _bdot(_bdot_nt(q, k) * decay_mask, v)
        inter = _bdot(q * q_decay, state)
        state_ref[h] = state * chunk_decay + _bdot_tn(k * k_decay, v)
        y = intra + inter
        mu = jnp.mean(y, axis=-1, keepdims=True)
        yc = y - mu
        var = jnp.mean(yc * yc, axis=-1, keepdims=True)
        o_ref[:, sl] = (_silu(g) * (yc * lax.rsqrt(var + 1e-6))).astype(o_ref.dtype)


def _retention(proj, pos, *, batch):
    m = proj.shape[0]
    nt = m // batch // CHUNK
    d = RET_DIM
    inv_freq = ROPE_BASE ** (-jnp.arange(0, d, 2, dtype=F32) / d)
    freq2 = jnp.concatenate([inv_freq, inv_freq]).reshape(1, d)
    sign = jnp.concatenate([-jnp.ones((d // 2,), F32), jnp.ones((d // 2,), F32)]).reshape(1, d)
    return pl.pallas_call(
        _retention_kernel,
        grid=(batch, nt),
        in_specs=[pl.BlockSpec((CHUNK, 4 * RET_WIDTH), lambda b, i: (b * nt + i, 0)),
                  pl.BlockSpec((CHUNK, 1), lambda b, i: (b * nt + i, 0)),
                  pl.BlockSpec((1, d), lambda b, i: (0, 0)),
                  pl.BlockSpec((1, d), lambda b, i: (0, 0))],
        out_specs=pl.BlockSpec((CHUNK, RET_WIDTH), lambda b, i: (b * nt + i, 0)),
        out_shape=jax.ShapeDtypeStruct((m, RET_WIDTH), BF16),
        scratch_shapes=[pltpu.VMEM((RET_HEADS, d, d), F32)],
        compiler_params=_cparams("parallel", "arbitrary"),
        name="retention",
    )(proj, pos, freq2, sign)


def _gdn_kernel(p_ref, conv_ref, alog_ref, dtb_ref, nw_ref, o_ref, xpad_ref, state_ref):
    c = CHUNK
    w3 = 3 * GDN_WIDTH
    halo = 8

    @pl.when(pl.program_id(1) == 0)
    def _():
        state_ref[...] = jnp.zeros_like(state_ref)
        xpad_ref[0:halo, :] = jnp.zeros((halo, w3), F32)

    @pl.when(pl.program_id(1) != 0)
    def _():
        xpad_ref[0:halo, :] = xpad_ref[c:c + halo, :]

    xpad_ref[halo:halo + c, :] = p_ref[:, 0:w3]
    conv = conv_ref[0:1, :] * xpad_ref[halo - 3:halo - 3 + c, :]
    for j in range(1, CONV_K):
        conv = conv + conv_ref[j:j + 1, :] * xpad_ref[halo - 3 + j:halo - 3 + j + c, :]
    qkv = _silu(conv)

    beta_all = _sigmoid(p_ref[:, 4 * GDN_WIDTH:4 * GDN_WIDTH + LANES])
    log_a = -jnp.exp(alog_ref[...]) * _softplus(p_ref[:, 4 * GDN_WIDTH + LANES:GDN_COLS] + dtb_ref[...])
    causal, strict = _tri_masks(c)
    tril01 = jnp.where(causal, 1.0, 0.0)
    cum_all = _dot_exact_lhs(tril01, log_a)
    cum_rows = cum_all.T

    heads = range(GDN_HEADS)
    head_cols = lambda group, h: slice(group * GDN_WIDTH + h * GDN_DIM, group * GDN_WIDTH + (h + 1) * GDN_DIM)
    qs = [qkv[:, head_cols(0, h)] for h in heads]
    ks = [qkv[:, head_cols(1, h)] for h in heads]
    vs = [qkv[:, head_cols(2, h)] for h in heads]
    qs = [q * lax.rsqrt(jnp.sum(q * q, axis=-1, keepdims=True) + 1e-6) * GDN_DIM ** -0.5 for q in qs]
    ks = [k * lax.rsqrt(jnp.sum(k * k, axis=-1, keepdims=True) + 1e-6) for k in ks]
    betas = [beta_all[:, h:h + 1] for h in heads]
    cums = [cum_all[:, h:h + 1] for h in heads]
    lasts = [cum_all[c - 1:c, h:h + 1] for h in heads]
    segs = [jnp.where(causal, jnp.exp(jnp.where(causal, cums[h] - cum_rows[h:h + 1, :], 0.0)), 0.0) for h in heads]
    kbs = [ks[h] * betas[h] for h in heads]
    ecums = [jnp.exp(cums[h]) for h in heads]
    scores = [_bdot_nt(jnp.concatenate([kbs[h], qs[h]], axis=0), ks[h]) for h in heads]
    t_invs = _unit_lower_inverses([jnp.where(strict, -scores[h][0:c] * segs[h], 0.0) for h in heads])
    qks = [scores[h][c:2 * c] * segs[h] for h in heads]
    uws = [_bdot(t_invs[h], jnp.concatenate([vs[h] * betas[h], kbs[h] * ecums[h]], axis=1)) for h in heads]
    states = [state_ref[h] for h in heads]
    wq_s = [_bdot(jnp.concatenate([uws[h][:, GDN_DIM:2 * GDN_DIM], qs[h] * ecums[h]], axis=0), states[h])
            for h in heads]
    v_news = [uws[h][:, 0:GDN_DIM] - wq_s[h][0:c] for h in heads]
    outs = [wq_s[h][c:2 * c] + _bdot(qks[h], v_news[h]) for h in heads]
    for h in heads:
        state_ref[h] = (states[h] * jnp.exp(lasts[h])
                        + _bdot_tn(ks[h] * jnp.exp(lasts[h] - cums[h]), v_news[h]))
    for h in heads:
        o = outs[h]
        o = o * lax.rsqrt(jnp.mean(o * o, axis=-1, keepdims=True) + 1e-6) * nw_ref[...]
        o_ref[:, head_cols(0, h)] = (o * _silu(p_ref[:, head_cols(3, h)])).astype(o_ref.dtype)


def _gated_deltanet(proj, conv_w, a_log, dt_bias, norm_w, *, batch):
    m = proj.shape[0]
    nt = m // batch // CHUNK
    pad = lambda a: jnp.pad(a.reshape(1, -1), ((0, 0), (0, LANES - a.shape[-1])))
    return pl.pallas_call(
        _gdn_kernel,
        grid=(batch, nt),
        in_specs=[pl.BlockSpec((CHUNK, GDN_COLS), lambda b, i: (b * nt + i, 0)),
                  pl.BlockSpec((CONV_K, 3 * GDN_WIDTH), lambda b, i: (0, 0)),
                  pl.BlockSpec((1, LANES), lambda b, i: (0, 0)),
                  pl.BlockSpec((1, LANES), lambda b, i: (0, 0)),
                  pl.BlockSpec((1, GDN_DIM), lambda b, i: (0, 0))],
        out_specs=pl.BlockSpec((CHUNK, GDN_WIDTH), lambda b, i: (b * nt + i, 0)),
        out_shape=jax.ShapeDtypeStruct((m, GDN_WIDTH), BF16),
        scratch_shapes=[pltpu.VMEM((CHUNK + 8, 3 * GDN_WIDTH), F32),
                        pltpu.VMEM((GDN_HEADS, GDN_DIM, GDN_DIM), F32)],
        compiler_params=_cparams("parallel", "arbitrary"),
        name="gated_deltanet",
    )(proj, conv_w, pad(a_log), pad(dt_bias), norm_w.reshape(1, GDN_DIM))


def _rwkv_kernel(p_ref, mu_ref, wup_ref, w0_ref, aup_ref, a0_ref, gup_ref, kk_ref, ka_ref, rk_ref,
                 lnw_ref, lnb_ref, o_ref, prev_ref, state_ref):
    c = CHUNK
    wd_ = RWKV_WIDTH
    halo = 8

    @pl.when(pl.program_id(1) == 0)
    def _():
        state_ref[...] = jnp.zeros_like(state_ref)
        prev_ref[0:halo, :] = jnp.zeros((halo, RWKV_IN), F32)

    @pl.when(pl.program_id(1) != 0)
    def _():
        prev_ref[0:halo, :] = prev_ref[c:c + halo, :]

    prev_ref[halo:halo + c, :] = p_ref[...]
    p = p_ref[...]
    p = p + (prev_ref[halo - 1:halo - 1 + c, :] - p) * mu_ref[...]

    r_all = p[:, 0:wd_]
    k_in = p[:, wd_:2 * wd_]
    v_all = p[:, 2 * wd_:3 * wd_]
    w_down = p[:, 3 * wd_:3 * wd_ + W_LORA]
    a_down = p[:, 3 * wd_ + W_LORA:3 * wd_ + W_LORA + A_LORA]
    g_down = p[:, 3 * wd_ + W_LORA + A_LORA:RWKV_IN]

    w_log = -_softplus(-(w0_ref[...] + _bdot(jnp.tanh(w_down), wup_ref[...]))) - 0.5
    log_w = -jnp.exp(w_log)
    a_sig = _sigmoid(a0_ref[...] + _bdot(a_down, aup_ref[...]))
    g_all = _bdot(_sigmoid(g_down), gup_ref[...])

    causal, strict = _tri_masks(c)
    cum = _dot_exact_lhs(jnp.where(causal, 1.0, 0.0), log_w)
    lane = lax.broadcasted_iota(jnp.int32, (c, LANES), 1)
    first = lane < RWKV_DIM
    rl = lax.broadcasted_iota(jnp.int32, (LANES, LANES), 0) < RWKV_DIM
    cl = lax.broadcasted_iota(jnp.int32, (LANES, LANES), 1) < RWKV_DIM
    same_head = rl == cl
    head_ones = jnp.where(same_head, 1.0, 0.0).astype(BF16)
    pairs = range(RWKV_PAIRS)
    pair_cols = lambda pr: slice(pr * LANES, (pr + 1) * LANES)

    def head_sum(y):
        return jnp.concatenate([_dot_split_lhs(y[:, pair_cols(pr)], head_ones) for pr in pairs], axis=1)

    kk = k_in * kk_ref[...]
    kk = kk * lax.rsqrt(head_sum(kk * kk) + 1e-12)
    k_all = k_in * (1.0 + (a_sig - 1.0) * ka_ref[...])
    a_vec = -kk
    b_vec = kk * a_sig
    ref_c = cum[c // 2 - 1:c // 2, :]
    last = cum[c - 1:c, :]
    e_out = jnp.exp(ref_c - cum)
    e_tail = jnp.exp(last - cum)
    a_in = (a_vec * jnp.exp(cum - log_w - ref_c)).astype(BF16)
    r_in = (r_all * jnp.exp(cum - ref_c)).astype(BF16)
    a_prev = (a_vec * jnp.exp(cum - log_w)).astype(BF16)
    r_cum = (r_all * jnp.exp(cum)).astype(BF16)
    b_out = (b_vec * e_out).astype(BF16)
    k_out = (k_all * e_out).astype(BF16)
    k_tail = (k_all * e_tail).astype(BF16)
    b_tail = (b_vec * e_tail).astype(BF16)
    v_bf = v_all.astype(BF16)
    e_last = jnp.exp(last)

    zero = jnp.zeros((), BF16)
    row2 = lax.broadcasted_iota(jnp.int32, (c, 2 * c), 0)
    col2 = lax.broadcasted_iota(jnp.int32, (c, 2 * c), 1)
    causal2 = row2 >= (col2 & (c - 1))
    strict_hi = jnp.logical_and(col2 >= c, row2 + c > col2)
    heads = range(RWKV_HEADS)
    own = [first if h % 2 == 0 else jnp.logical_not(first) for h in heads]

    outs_p = [jnp.concatenate([b_out[:, pair_cols(pr)], k_out[:, pair_cols(pr)]], axis=0) for pr in pairs]
    scores = []
    for h in heads:
        cols = pair_cols(h // 2)
        x = jnp.concatenate([jnp.where(own[h], a_in[:, cols], zero), jnp.where(own[h], r_in[:, cols], zero)], axis=0)
        scores.append(lax.dot_general(x, outs_p[h // 2], (((1,), (1,)), ((), ())), preferred_element_type=F32))
    t_invs = _unit_lower_inverses([jnp.where(strict, scores[h][0:c, 0:c], 0.0) for h in heads])
    ak_hi = [jnp.where(strict_hi, scores[h][0:c, :], 0.0).astype(BF16) for h in heads]
    rbk = [jnp.where(causal2, scores[h][c:2 * c, :], 0.0).astype(BF16) for h in heads]
    vv = [jnp.concatenate([v_bf[:, pair_cols(pr)], v_bf[:, pair_cols(pr)]], axis=0) for pr in pairs]
    akv_h = [jnp.dot(ak_hi[h], vv[h // 2], preferred_element_type=F32) for h in heads]
    states = [state_ref[pr] for pr in pairs]
    read = [lax.dot_general(jnp.concatenate([a_prev[:, pair_cols(pr)], r_cum[:, pair_cols(pr)]], axis=0),
                            states[pr].astype(BF16), (((1,), (1,)), ((), ())), preferred_element_type=F32)
            for pr in pairs]
    rhs = [read[pr][0:c] + jnp.where(first, akv_h[2 * pr], akv_h[2 * pr + 1]) for pr in pairs]
    u_h = [_bdot(t_invs[h], rhs[h // 2]) for h in heads]
    u_p = [jnp.where(first, u_h[2 * pr], u_h[2 * pr + 1]) for pr in pairs]
    uv = [jnp.concatenate([u_p[pr].astype(BF16), v_bf[:, pair_cols(pr)]], axis=0) for pr in pairs]
    y_h = [jnp.dot(rbk[h], uv[h // 2], preferred_element_type=F32) for h in heads]
    y_all = jnp.concatenate([read[pr][c:2 * c] + jnp.where(first, y_h[2 * pr], y_h[2 * pr + 1]) for pr in pairs],
                            axis=1)
    for pr in pairs:
        cols = pair_cols(pr)
        upd = lax.dot_general(jnp.concatenate([v_bf[:, cols], u_p[pr].astype(BF16)], axis=0),
                              jnp.concatenate([k_tail[:, cols], b_tail[:, cols]], axis=0),
                              (((0,), (0,)), ((), ())), preferred_element_type=F32)
        state_ref[pr] = states[pr] * e_last[:, cols] + jnp.where(same_head, upd, 0.0)

    mean = head_sum(y_all) * (1.0 / RWKV_DIM)
    yc = y_all - mean
    var = head_sum(yc * yc) * (1.0 / RWKV_DIM)
    yn = yc * lax.rsqrt(var + RWKV_LNX_EPS) * lnw_ref[...] + lnb_ref[...]
    yn = yn + head_sum(r_all * k_all * rk_ref[...]) * v_all
    o_ref[...] = (yn * g_all).astype(o_ref.dtype)


def _rwkv7(proj, mu, w_up, w0, a_up, a0, g_up, k_k, k_a, r_k, lnx_w, lnx_b, *, batch):
    m = proj.shape[0]
    nt = m // batch // CHUNK
    row = lambda a: a.reshape(1, -1)
    full = lambda a: pl.BlockSpec(a.shape, lambda b, i: (0, 0))
    params = [row(mu), w_up.astype(BF16), row(w0), a_up.astype(BF16), row(a0), g_up.astype(BF16),
              row(k_k), row(k_a), row(r_k), row(lnx_w), row(lnx_b)]
    return pl.pallas_call(
        _rwkv_kernel,
        grid=(batch, nt),
        in_specs=[pl.BlockSpec((CHUNK, RWKV_IN), lambda b, i: (b * nt + i, 0))] + [full(a) for a in params],
        out_specs=pl.BlockSpec((CHUNK, RWKV_WIDTH), lambda b, i: (b * nt + i, 0)),
        out_shape=jax.ShapeDtypeStruct((m, RWKV_WIDTH), BF16),
        scratch_shapes=[pltpu.VMEM((CHUNK + 8, RWKV_IN), F32),
                        pltpu.VMEM((RWKV_PAIRS, LANES, LANES), F32)],
        compiler_params=_cparams("parallel", "arbitrary"),
        name="rwkv7",
    )(proj, *params)


def _split_w_in(w_in):
    g3 = 3 * GDN_WIDTH
    o_gate = g3
    o_beta = o_gate + GDN_WIDTH
    o_alpha = o_beta + GDN_HEADS
    o_ret = o_alpha + GDN_HEADS
    o_rwkv = o_ret + 4 * RET_WIDTH
    zpad = jnp.zeros((w_in.shape[0], LANES - GDN_HEADS), w_in.dtype)
    w_gdn = jnp.concatenate([w_in[:, :o_beta], w_in[:, o_beta:o_alpha], zpad,
                             w_in[:, o_alpha:o_ret], zpad], axis=1)
    return (w_gdn.astype(BF16), w_in[:, o_ret:o_rwkv].astype(BF16), w_in[:, o_rwkv:].astype(BF16))


def kernel(x, mem, positions, w_in, gdn_conv, gdn_a_log, gdn_dt_bias, gdn_norm, rwkv_mu, rwkv_w_up, rwkv_w0, rwkv_a_up, rwkv_a0, rwkv_g_up, rwkv_k_k, rwkv_k_a, rwkv_r_k, rwkv_lnx_w, rwkv_lnx_b, w_out, ln1_g, ln1_b, xattn_q, xattn_k, xattn_v, xattn_o, ln2_g, ln2_b, ffn_gate_up, ffn_down, ln3_g, ln3_b):
    batch, seq, d = x.shape
    m = batch * seq
    tm = min(512, m)
    xf = x.reshape(m, d)
    xb = xf.astype(BF16)
    memb = mem.reshape(batch * N_MEM, d).astype(BF16)
    pos = positions.astype(F32).reshape(m, 1)

    for l in range(DEPTH):
        w_gdn, w_ret, w_rwkv = _split_w_in(w_in[l])
        p_gdn = _matmul(xb, w_gdn, tm=tm, tn=GDN_COLS // 2, out_dtype=F32)
        p_ret = _matmul(xb, w_ret, tm=tm, tn=4 * RET_WIDTH // 2, out_dtype=F32)
        p_rwkv = _matmul(xb, w_rwkv, tm=tm, tn=RWKV_IN // 2, out_dtype=F32)
        y_a = _gated_deltanet(p_gdn, gdn_conv[l], gdn_a_log[l], gdn_dt_bias[l], gdn_norm[l], batch=batch)
        y_b = _retention(p_ret, pos, batch=batch)
        y_c = _rwkv7(p_rwkv, rwkv_mu[l], rwkv_w_up[l], rwkv_w0[l], rwkv_a_up[l], rwkv_a0[l], rwkv_g_up[l],
                     rwkv_k_k[l], rwkv_k_a[l], rwkv_r_k[l], rwkv_lnx_w[l], rwkv_lnx_b[l], batch=batch)
        y = jnp.concatenate([y_a, y_b, y_c], axis=-1)
        xf, xb = _matmul_residual_ln(y, w_out[l].astype(BF16), xf, ln1_g[l], ln1_b[l], tm=tm, tk=d)

        q = _matmul(xb, xattn_q[l].astype(BF16), tm=tm, tn=d // 2, out_dtype=BF16)
        kmem = _matmul(memb, xattn_k[l].astype(BF16), tm=tm, tn=d // 2, out_dtype=BF16)
        vmem = _matmul(memb, xattn_v[l].astype(BF16), tm=tm, tn=d // 2, out_dtype=BF16)
        att = _xattn_core(q, kmem, vmem, batch=batch, tq=min(512, seq))
        xf, xb = _matmul_residual_ln(att, xattn_o[l].astype(BF16), xf, ln2_g[l], ln2_b[l], tm=tm, tk=d)

        hidden = _swiglu_up(xb, ffn_gate_up[l].astype(BF16), tm=tm, tn=512)
        xf, xb = _matmul_residual_ln(hidden, ffn_down[l].astype(BF16), xf, ln3_g[l], ln3_b[l], tm=tm, tk=512)

    return xf.reshape(batch, seq, d)
```

```python
import functools
import math

import jax
import jax.numpy as jnp
from jax import lax
from jax.experimental import pallas as pl
from jax.experimental.pallas import tpu as pltpu

F32 = jnp.float32
BF16 = jnp.bfloat16

D_MODEL = 2048
DEPTH = 2
N_MEM = 256
XATTN_HEADS = 4
XATTN_DIM = D_MODEL // XATTN_HEADS

GDN_DIM = 128
GDN_HEADS = 6
GDN_WIDTH = GDN_HEADS * GDN_DIM
CONV_K = 4

RET_DIM = 128
RET_HEADS = 4
RET_WIDTH = RET_HEADS * RET_DIM
ROPE_BASE = 10000.0

RWKV_DIM = 64
RWKV_HEADS = 12
RWKV_WIDTH = RWKV_HEADS * RWKV_DIM
RWKV_PAIRS = RWKV_HEADS // 2
W_LORA = 64
A_LORA = 64
G_LORA = 128
RWKV_LNX_EPS = 64e-5
RWKV_IN = 3 * RWKV_WIDTH + W_LORA + A_LORA + G_LORA

IN_WIDTH = 4 * GDN_WIDTH + 2 * GDN_HEADS + 4 * RET_WIDTH + RWKV_IN
CHUNK = 64
MIXER_CHUNKS = 4
HALO = 8
LANES = 128
DEEPNORM_ALPHA = (2 * DEPTH) ** 0.25
FFN_DOWN_K_STEPS = 4

GDN_COLS = 4 * GDN_WIDTH + 2 * LANES
VMEM_LIMIT = 56 * 1024 * 1024


def _cparams(*sem):
    return pltpu.CompilerParams(dimension_semantics=sem, vmem_limit_bytes=VMEM_LIMIT)


def _bdot(a, b):
    return jnp.dot(a.astype(BF16), b.astype(BF16), preferred_element_type=F32)


def _bdot_nt(a, b):
    return lax.dot_general(a.astype(BF16), b.astype(BF16), (((1,), (1,)), ((), ())),
                           preferred_element_type=F32)


def _bdot_tn(a, b):
    return lax.dot_general(a.astype(BF16), b.astype(BF16), (((0,), (0,)), ((), ())),
                           preferred_element_type=F32)


def _cumsum_rows(tril01, a):
    hi = a.astype(BF16)
    lo = (a - hi.astype(F32)).astype(BF16)
    return jnp.dot(tril01, hi, preferred_element_type=F32) + jnp.dot(tril01, lo, preferred_element_type=F32)


def _unit_lower_inverses(ws):
    c = ws[0].shape[0]
    right = lax.broadcasted_iota(jnp.int32, (c, 2 * c), 1) >= c
    for _ in range(int(math.log2(c))):
        prods = [_bdot(w[:, 0:c], w) for w in ws]
        ws = [prod + jnp.where(right, w, 0.0) for prod, w in zip(prods, ws)]
    return ws


def _tri_masks(c):
    rows = lax.broadcasted_iota(jnp.int32, (c, c), 0)
    cols = lax.broadcasted_iota(jnp.int32, (c, c), 1)
    return rows >= cols, rows > cols


def _sigmoid(x):
    return 1.0 / (1.0 + jnp.exp(-x))


def _silu(x):
    return x * _sigmoid(x)


def _softplus(x):
    return jnp.maximum(x, 0.0) + jnp.log1p(jnp.exp(-jnp.abs(x)))


def _chunk_rows(i):
    return pl.ds(pl.multiple_of(i * CHUNK, CHUNK), CHUNK)


def _regroup_kernel(w_ref, gdn_ref, ret_ref, rwkv_ref):
    w = w_ref[...]
    o_beta = 4 * GDN_WIDTH
    o_alpha = o_beta + GDN_HEADS
    o_ret = o_alpha + GDN_HEADS
    o_rwkv = o_ret + 4 * RET_WIDTH
    lane = lax.broadcasted_iota(jnp.int32, (w.shape[0], LANES), 1)
    gdn_ref[:, 0:o_beta] = w[:, 0:o_beta].astype(BF16)
    gdn_ref[:, o_beta:o_beta + LANES] = jnp.where(lane < GDN_HEADS, w[:, o_beta:o_beta + LANES], 0.0).astype(BF16)
    gdn_ref[:, o_beta + LANES:GDN_COLS] = jnp.where(lane < GDN_HEADS, w[:, o_alpha:o_alpha + LANES], 0.0).astype(BF16)
    ret_ref[...] = w[:, o_ret:o_rwkv].astype(BF16)
    rwkv_ref[...] = w[:, o_rwkv:IN_WIDTH].astype(BF16)


def _regroup_w_in(w_in, *, tk):
    depth, k, n = w_in.shape
    assert n == IN_WIDTH and k % tk == 0
    widths = (GDN_COLS, 4 * RET_WIDTH, RWKV_IN)
    return pl.pallas_call(
        _regroup_kernel,
        grid=(depth, k // tk),
        in_specs=[pl.BlockSpec((None, tk, n), lambda l, i: (l, i, 0))],
        out_specs=[pl.BlockSpec((None, tk, wd), lambda l, i: (l, i, 0)) for wd in widths],
        out_shape=[jax.ShapeDtypeStruct((depth, k, wd), BF16) for wd in widths],
        compiler_params=_cparams("parallel", "arbitrary"),
        name="regroup_w_in",
    )(w_in)


def _mm_kernel(x_ref, w_ref, o_ref):
    o_ref[...] = jnp.dot(x_ref[...].astype(BF16), w_ref[...], preferred_element_type=F32).astype(o_ref.dtype)


def _matmul(x, w, *, tm, tn, out_dtype):
    m, k = x.shape
    n = w.shape[1]
    assert m % tm == 0 and n % tn == 0
    return pl.pallas_call(
        _mm_kernel,
        grid=(m // tm, n // tn),
        in_specs=[pl.BlockSpec((tm, k), lambda i, j: (i, 0)),
                  pl.BlockSpec((k, tn), lambda i, j: (0, j))],
        out_specs=pl.BlockSpec((tm, tn), lambda i, j: (i, j)),
        out_shape=jax.ShapeDtypeStruct((m, n), out_dtype),
        compiler_params=_cparams("parallel", "arbitrary"),
        name="proj_matmul",
    )(x, w)


def _mm_ln_kernel(*refs, n_x, nk):
    x_refs = refs[:n_x]
    w_ref, res_ref, g_ref, b_ref, of_ref, ob_ref, acc_ref = refs[n_x:]
    k = pl.program_id(1)

    @pl.when(k == 0)
    def _():
        acc_ref[...] = jnp.zeros_like(acc_ref)

    row = 0
    for x_ref in x_refs:
        width = x_ref.shape[1]
        acc_ref[...] += jnp.dot(x_ref[...], w_ref[row:row + width, :], preferred_element_type=F32)
        row += width

    @pl.when(k == nk - 1)
    def _():
        y = DEEPNORM_ALPHA * res_ref[...] + acc_ref[...]
        mu = jnp.mean(y, axis=-1, keepdims=True)
        yc = y - mu
        var = jnp.mean(yc * yc, axis=-1, keepdims=True)
        out = yc * lax.rsqrt(var + 1e-5) * g_ref[...] + b_ref[...]
        of_ref[...] = out
        ob_ref[...] = out.astype(BF16)


def _matmul_residual_ln(xs, w, res, g, b, *, tm, tk):
    m = res.shape[0]
    kdim, n = w.shape
    assert m % tm == 0 and kdim % tk == 0 and sum(x.shape[1] for x in xs) == kdim
    nk = kdim // tk
    assert len(xs) == 1 or nk == 1
    x_specs = [pl.BlockSpec((tm, tk if len(xs) == 1 else x.shape[1]), lambda i, k: (i, k)) for x in xs]
    return pl.pallas_call(
        functools.partial(_mm_ln_kernel, n_x=len(xs), nk=nk),
        grid=(m // tm, nk),
        in_specs=x_specs + [pl.BlockSpec((tk, n), lambda i, k: (k, 0)),
                            pl.BlockSpec((tm, n), lambda i, k: (i, 0)),
                            pl.BlockSpec((1, n), lambda i, k: (0, 0)),
                            pl.BlockSpec((1, n), lambda i, k: (0, 0))],
        out_specs=[pl.BlockSpec((tm, n), lambda i, k: (i, 0)),
                   pl.BlockSpec((tm, n), lambda i, k: (i, 0))],
        out_shape=[jax.ShapeDtypeStruct((m, n), F32), jax.ShapeDtypeStruct((m, n), BF16)],
        scratch_shapes=[pltpu.VMEM((tm, n), F32)],
        compiler_params=_cparams("parallel", "arbitrary"),
        name="proj_residual_layernorm",
    )(*xs, w, res, g.reshape(1, n), b.reshape(1, n))


def _swiglu_kernel(x_ref, wg_ref, wu_ref, o_ref):
    x = x_ref[...]
    gate = jnp.dot(x, wg_ref[...], preferred_element_type=F32)
    up = jnp.dot(x, wu_ref[...], preferred_element_type=F32)
    o_ref[...] = (_silu(gate) * up).astype(o_ref.dtype)


def _swiglu_up(x, w_gate_up, *, tm, tn):
    m, k = x.shape
    h = w_gate_up.shape[1] // 2
    assert m % tm == 0 and h % tn == 0
    nj = h // tn
    return pl.pallas_call(
        _swiglu_kernel,
        grid=(m // tm, nj),
        in_specs=[pl.BlockSpec((tm, k), lambda i, j: (i, 0)),
                  pl.BlockSpec((k, tn), lambda i, j: (0, j)),
                  pl.BlockSpec((k, tn), lambda i, j: (0, j + nj))],
        out_specs=pl.BlockSpec((tm, tn), lambda i, j: (i, j)),
        out_shape=jax.ShapeDtypeStruct((m, h), BF16),
        compiler_params=_cparams("parallel", "arbitrary"),
        name="swiglu_up",
    )(x, w_gate_up, w_gate_up)


def _xattn_kernel(q_ref, k_ref, v_ref, o_ref):
    scale = XATTN_DIM ** -0.5
    for h in range(XATTN_HEADS):
        sl = slice(h * XATTN_DIM, (h + 1) * XATTN_DIM)
        s = lax.dot_general(q_ref[:, sl], k_ref[:, sl], (((1,), (1,)), ((), ())),
                            preferred_element_type=F32) * scale
        s = s - jnp.max(s, axis=-1, keepdims=True)
        p = jnp.exp(s)
        p = p / jnp.sum(p, axis=-1, keepdims=True)
        o_ref[:, sl] = jnp.dot(p.astype(BF16), v_ref[:, sl], preferred_element_type=F32).astype(o_ref.dtype)


def _xattn_core(q, k, v, *, batch, tq):
    m = q.shape[0]
    t = m // batch
    assert t % tq == 0
    nt = t // tq
    return pl.pallas_call(
        _xattn_kernel,
        grid=(batch, nt),
        in_specs=[pl.BlockSpec((tq, D_MODEL), lambda b, i: (b * nt + i, 0)),
                  pl.BlockSpec((N_MEM, D_MODEL), lambda b, i: (b, 0)),
                  pl.BlockSpec((N_MEM, D_MODEL), lambda b, i: (b, 0))],
        out_specs=pl.BlockSpec((tq, D_MODEL), lambda b, i: (b * nt + i, 0)),
        out_shape=jax.ShapeDtypeStruct((m, D_MODEL), BF16),
        compiler_params=_cparams("parallel", "arbitrary"),
        name="xattn_core",
    )(q, k, v)


def _mixer_call(kernel_fn, proj, extra, extra_specs, out_width, scratch, *, batch, name):
    m, cols = proj.shape
    tb = min(MIXER_CHUNKS * CHUNK, m // batch)
    nt = m // batch // tb
    assert m == batch * nt * tb and tb % CHUNK == 0
    return pl.pallas_call(
        functools.partial(kernel_fn, nchunks=tb // CHUNK),
        grid=(batch, nt),
        in_specs=[pl.BlockSpec((tb, cols), lambda b, i: (b * nt + i, 0))] + extra_specs(tb, nt),
        out_specs=pl.BlockSpec((tb, out_width), lambda b, i: (b * nt + i, 0)),
        out_shape=jax.ShapeDtypeStruct((m, out_width), BF16),
        scratch_shapes=scratch(tb),
        compiler_params=_cparams("parallel", "arbitrary"),
        name=name,
    )(proj, *extra)


def _const_spec(a):
    return pl.BlockSpec(a.shape, lambda b, i: (0,) * a.ndim)


def _retention_kernel(p_ref, pos_ref, freq_ref, sign_ref, o_ref, state_ref, *, nchunks):
    c = CHUNK

    @pl.when(pl.program_id(1) == 0)
    def _():
        state_ref[...] = jnp.zeros_like(state_ref)

    heads = range(RET_HEADS)
    head_cols = lambda group, h: slice(group * RET_WIDTH + h * RET_DIM, group * RET_WIDTH + (h + 1) * RET_DIM)
    rows = lax.broadcasted_iota(jnp.int32, (c, c), 0)
    cols = lax.broadcasted_iota(jnp.int32, (c, c), 1)
    rel = (rows - cols).astype(F32)
    tpos = lax.broadcasted_iota(jnp.int32, (c, RET_DIM), 0).astype(F32)
    log_gammas = [math.log1p(-2.0 ** (-5.0 - h)) for h in heads]
    decay_masks = [jnp.where(rel >= 0, jnp.exp(jnp.where(rel >= 0, rel, 0.0) * lg), 0.0) for lg in log_gammas]
    q_decays = [jnp.exp((tpos + 1.0) * lg) for lg in log_gammas]
    k_decays = [jnp.exp((c - 1.0 - tpos) * lg) for lg in log_gammas]

    def chunk(i, carry):
        r = _chunk_rows(i)
        ang = pos_ref[r, :] * freq_ref[...]
        cos = jnp.cos(ang)
        sin_signed = jnp.sin(ang) * sign_ref[...]

        def rotary(y):
            return y * cos + pltpu.roll(y, RET_DIM // 2, 1) * sin_signed

        qs = [rotary(p_ref[r, head_cols(0, h)]) for h in heads]
        ks = [rotary(p_ref[r, head_cols(1, h)]) * RET_DIM ** -0.5 for h in heads]
        vs = [p_ref[r, head_cols(2, h)].astype(BF16) for h in heads]
        scores = [_bdot_nt(qs[h], ks[h]) * decay_masks[h] for h in heads]
        states = [state_ref[h] for h in heads]
        ys = [_bdot(scores[h], vs[h]) + _bdot(qs[h] * q_decays[h], states[h]) for h in heads]
        for h in heads:
            state_ref[h] = states[h] * math.exp(c * log_gammas[h]) + _bdot_tn(ks[h] * k_decays[h], vs[h])
        for h in heads:
            y = ys[h]
            mu = jnp.mean(y, axis=-1, keepdims=True)
            yc = y - mu
            var = jnp.mean(yc * yc, axis=-1, keepdims=True)
            o_ref[r, head_cols(0, h)] = (_silu(p_ref[r, head_cols(3, h)])
                                         * (yc * lax.rsqrt(var + 1e-6))).astype(o_ref.dtype)
        return carry

    lax.fori_loop(0, nchunks, chunk, 0)


def _retention(proj, pos, *, batch):
    d = RET_DIM
    inv_freq = ROPE_BASE ** (-jnp.arange(0, d, 2, dtype=F32) / d)
    freq2 = jnp.concatenate([inv_freq, inv_freq]).reshape(1, d)
    sign = jnp.concatenate([-jnp.ones((d // 2,), F32), jnp.ones((d // 2,), F32)]).reshape(1, d)
    return _mixer_call(
        _retention_kernel, proj, (pos, freq2, sign),
        lambda tb, nt: [pl.BlockSpec((tb, 1), lambda b, i: (b * nt + i, 0)), _const_spec(freq2), _const_spec(sign)],
        RET_WIDTH, lambda tb: [pltpu.VMEM((RET_HEADS, d, d), F32)], batch=batch, name="retention")


def _gdn_kernel(p_ref, conv_ref, alog_ref, dtb_ref, nw_ref, o_ref, xpad_ref, qkv_ref, state_ref, *, nchunks):
    c = CHUNK
    tb = nchunks * c
    w3 = 3 * GDN_WIDTH

    @pl.when(pl.program_id(1) == 0)
    def _():
        state_ref[...] = jnp.zeros_like(state_ref)
        xpad_ref[0:HALO, :] = jnp.zeros((HALO, w3), F32)

    @pl.when(pl.program_id(1) != 0)
    def _():
        xpad_ref[0:HALO, :] = xpad_ref[tb:tb + HALO, :]

    xpad_ref[HALO:HALO + tb, :] = p_ref[:, 0:w3]
    first_tap = HALO - (CONV_K - 1)
    conv = conv_ref[0:1, :] * xpad_ref[first_tap:first_tap + tb, :]
    for j in range(1, CONV_K):
        conv = conv + conv_ref[j:j + 1, :] * xpad_ref[first_tap + j:first_tap + j + tb, :]
    qkv_ref[...] = _silu(conv)

    causal, strict = _tri_masks(c)
    tril01 = jnp.where(causal, 1.0, 0.0).astype(BF16)
    eye = jnp.where(lax.broadcasted_iota(jnp.int32, (c, c), 0) == lax.broadcasted_iota(jnp.int32, (c, c), 1),
                    1.0, 0.0)
    right = lax.broadcasted_iota(jnp.int32, (c, 2 * c), 1) >= c
    heads = range(GDN_HEADS)
    head_cols = lambda group, h: slice(group * GDN_WIDTH + h * GDN_DIM, group * GDN_WIDTH + (h + 1) * GDN_DIM)

    def chunk(i, carry):
        r = _chunk_rows(i)
        beta_all = _sigmoid(p_ref[r, 4 * GDN_WIDTH:4 * GDN_WIDTH + LANES])
        log_a = -jnp.exp(alog_ref[...]) * _softplus(p_ref[r, 4 * GDN_WIDTH + LANES:GDN_COLS] + dtb_ref[...])
        cum_all = _cumsum_rows(tril01, log_a)
        cum_rows = cum_all.T
        qs = [qkv_ref[r, head_cols(0, h)] for h in heads]
        ks = [qkv_ref[r, head_cols(1, h)] for h in heads]
        vs = [qkv_ref[r, head_cols(2, h)] for h in heads]
        qs = [q * lax.rsqrt(jnp.sum(q * q, axis=-1, keepdims=True) + 1e-6) * GDN_DIM ** -0.5 for q in qs]
        ks = [k * lax.rsqrt(jnp.sum(k * k, axis=-1, keepdims=True) + 1e-6) for k in ks]
        betas = [beta_all[:, h:h + 1] for h in heads]
        cums = [cum_all[:, h:h + 1] for h in heads]
        lasts = [cum_all[c - 1:c, h:h + 1] for h in heads]
        segs = [jnp.where(causal, jnp.exp(jnp.where(causal, cums[h] - cum_rows[h:h + 1, :], 0.0)), 0.0)
                for h in heads]
        kbs = [ks[h] * betas[h] for h in heads]
        ecums = [jnp.exp(cums[h]) for h in heads]
        scores = [_bdot_nt(jnp.concatenate([kbs[h], qs[h]], axis=0), ks[h]) for h in heads]
        t_invs = _unit_lower_inverses([jnp.concatenate([jnp.where(strict, -scores[h][0:c] * segs[h], 0.0), eye],
                                                       axis=1) for h in heads])
        qks = [scores[h][c:2 * c] * segs[h] for h in heads]
        uws = []
        for h in heads:
            rhs = jnp.concatenate([vs[h] * betas[h], kbs[h] * ecums[h]], axis=1).astype(BF16)
            uws.append(_bdot(jnp.where(right, t_invs[h], 0.0), jnp.concatenate([rhs, rhs], axis=0)))
        states = [state_ref[h] for h in heads]
        wq_s = [_bdot(jnp.concatenate([uws[h][:, GDN_DIM:2 * GDN_DIM], qs[h] * ecums[h]], axis=0), states[h])
                for h in heads]
        v_news = [uws[h][:, 0:GDN_DIM] - wq_s[h][0:c] for h in heads]
        outs = [wq_s[h][c:2 * c] + _bdot(qks[h], v_news[h]) for h in heads]
        for h in heads:
            state_ref[h] = (states[h] * jnp.exp(lasts[h])
                            + _bdot_tn(ks[h] * jnp.exp(lasts[h] - cums[h]), v_news[h]))
        for h in heads:
            o = outs[h]
            o = o * lax.rsqrt(jnp.mean(o * o, axis=-1, keepdims=True) + 1e-6) * nw_ref[...]
            o_ref[r, head_cols(0, h)] = (o * _silu(p_ref[r, head_cols(3, h)])).astype(o_ref.dtype)
        return carry

    lax.fori_loop(0, nchunks, chunk, 0)


def _gated_deltanet(proj, conv_w, a_log, dt_bias, norm_w, *, batch):
    pad = lambda a: jnp.pad(a.reshape(1, -1), ((0, 0), (0, LANES - a.shape[-1])))
    extra = (conv_w, pad(a_log), pad(dt_bias), norm_w.reshape(1, GDN_DIM))
    return _mixer_call(
        _gdn_kernel, proj, extra, lambda tb, nt: [_const_spec(a) for a in extra], GDN_WIDTH,
        lambda tb: [pltpu.VMEM((tb + HALO, 3 * GDN_WIDTH), F32), pltpu.VMEM((tb, 3 * GDN_WIDTH), F32),
                    pltpu.VMEM((GDN_HEADS, GDN_DIM, GDN_DIM), F32)],
        batch=batch, name="gated_deltanet")


def _rwkv_kernel(p_ref, mu_ref, wup_ref, w0_ref, aup_ref, a0_ref, gup_ref, kk_ref, ka_ref, rk_ref,
                 lnw_ref, lnb_ref, o_ref, prev_ref, rkv_ref, logw_ref, asig_ref, gate_ref, state_ref, *, nchunks):
    c = CHUNK
    tb = nchunks * c
    wd_ = RWKV_WIDTH

    @pl.when(pl.program_id(1) == 0)
    def _():
        state_ref[...] = jnp.zeros_like(state_ref)
        prev_ref[0:HALO, :] = jnp.zeros((HALO, RWKV_IN), F32)

    @pl.when(pl.program_id(1) != 0)
    def _():
        prev_ref[0:HALO, :] = prev_ref[tb:tb + HALO, :]

    prev_ref[HALO:HALO + tb, :] = p_ref[...]
    p = p_ref[...]
    p = p + (prev_ref[HALO - 1:HALO - 1 + tb, :] - p) * mu_ref[...]
    rkv_ref[...] = p[:, 0:3 * wd_]
    w_down = p[:, 3 * wd_:3 * wd_ + W_LORA]
    a_down = p[:, 3 * wd_ + W_LORA:3 * wd_ + W_LORA + A_LORA]
    g_down = p[:, 3 * wd_ + W_LORA + A_LORA:RWKV_IN]
    w_log = -_softplus(-(w0_ref[...] + _bdot(jnp.tanh(w_down), wup_ref[...]))) - 0.5
    logw_ref[...] = -jnp.exp(w_log)
    asig_ref[...] = _sigmoid(a0_ref[...] + _bdot(a_down, aup_ref[...]))
    gate_ref[...] = _bdot(_sigmoid(g_down), gup_ref[...])

    causal, strict = _tri_masks(c)
    tril01 = jnp.where(causal, 1.0, 0.0).astype(BF16)
    lane = lax.broadcasted_iota(jnp.int32, (c, LANES), 1)
    first = lane < RWKV_DIM
    rl = lax.broadcasted_iota(jnp.int32, (LANES, LANES), 0) < RWKV_DIM
    cl = lax.broadcasted_iota(jnp.int32, (LANES, LANES), 1) < RWKV_DIM
    same_head = rl == cl
    head_ones = jnp.where(same_head, 1.0, 0.0).astype(BF16)
    row2 = lax.broadcasted_iota(jnp.int32, (c, 2 * c), 0)
    col2 = lax.broadcasted_iota(jnp.int32, (c, 2 * c), 1)
    causal2 = row2 >= (col2 & (c - 1))
    right = col2 >= c
    strict_lo = row2 > col2
    strict_hi = jnp.logical_and(right, row2 + c > col2)
    eye_hi = jnp.where(col2 == row2 + c, 1.0, 0.0)
    pairs = range(RWKV_PAIRS)
    heads = range(RWKV_HEADS)
    pair_cols = lambda pr: slice(pr * LANES, (pr + 1) * LANES)
    own = [first if h % 2 == 0 else jnp.logical_not(first) for h in heads]
    zero = jnp.zeros((), BF16)

    def head_sum(y):
        return jnp.concatenate([_bdot(y[:, pair_cols(pr)], head_ones) for pr in pairs], axis=1)

    def chunk(i, carry):
        r = _chunk_rows(i)
        r_all = rkv_ref[r, 0:wd_]
        k_in = rkv_ref[r, wd_:2 * wd_]
        v_all = rkv_ref[r, 2 * wd_:3 * wd_]
        log_w = logw_ref[r, :]
        a_sig = asig_ref[r, :]
        cum = _cumsum_rows(tril01, log_w)

        kk = k_in * kk_ref[...]
        kk = kk * lax.rsqrt(head_sum(kk * kk) + 1e-12)
        k_all = k_in * (1.0 + (a_sig - 1.0) * ka_ref[...])
        a_vec = -kk
        b_vec = kk * a_sig
        ref_c = cum[c // 2 - 1:c // 2, :]
        last = cum[c - 1:c, :]
        e_out = jnp.exp(ref_c - cum)
        e_tail = jnp.exp(last - cum)
        a_in = (a_vec * jnp.exp(cum - log_w - ref_c)).astype(BF16)
        r_in = (r_all * jnp.exp(cum - ref_c)).astype(BF16)
        a_prev = (a_vec * jnp.exp(cum - log_w)).astype(BF16)
        r_cum = (r_all * jnp.exp(cum)).astype(BF16)
        b_out = (b_vec * e_out).astype(BF16)
        k_out = (k_all * e_out).astype(BF16)
        k_tail = (k_all * e_tail).astype(BF16)
        b_tail = (b_vec * e_tail).astype(BF16)
        v_bf = v_all.astype(BF16)
        e_last = jnp.exp(last)

        outs_p = [jnp.concatenate([b_out[:, pair_cols(pr)], k_out[:, pair_cols(pr)]], axis=0) for pr in pairs]
        scores = []
        for h in heads:
            cols = pair_cols(h // 2)
            x = jnp.concatenate([jnp.where(own[h], a_in[:, cols], zero), jnp.where(own[h], r_in[:, cols], zero)],
                                axis=0)
            scores.append(lax.dot_general(x, outs_p[h // 2], (((1,), (1,)), ((), ())), preferred_element_type=F32))
        t_invs = _unit_lower_inverses([jnp.where(strict_lo, scores[h][0:c, :], eye_hi) for h in heads])
        ak_hi = [jnp.where(strict_hi, scores[h][0:c, :], 0.0).astype(BF16) for h in heads]
        rbk = [jnp.where(causal2, scores[h][c:2 * c, :], 0.0).astype(BF16) for h in heads]
        vv = [jnp.concatenate([v_bf[:, pair_cols(pr)], v_bf[:, pair_cols(pr)]], axis=0) for pr in pairs]
        akv_h = [jnp.dot(ak_hi[h], vv[h // 2], preferred_element_type=F32) for h in heads]
        states = [state_ref[pr] for pr in pairs]
        read = [lax.dot_general(jnp.concatenate([a_prev[:, pair_cols(pr)], r_cum[:, pair_cols(pr)]], axis=0),
                                states[pr].astype(BF16), (((1,), (1,)), ((), ())), preferred_element_type=F32)
                for pr in pairs]
        rhs = [read[pr][0:c] + jnp.where(first, akv_h[2 * pr], akv_h[2 * pr + 1]) for pr in pairs]
        rhs2 = [jnp.concatenate([rhs[pr].astype(BF16)] * 2, axis=0) for pr in pairs]
        u_h = [_bdot(jnp.where(right, t_invs[h], 0.0), rhs2[h // 2]) for h in heads]
        u_p = [jnp.where(first, u_h[2 * pr], u_h[2 * pr + 1]) for pr in pairs]
        uv = [jnp.concatenate([u_p[pr].astype(BF16), v_bf[:, pair_cols(pr)]], axis=0) for pr in pairs]
        y_h = [jnp.dot(rbk[h], uv[h // 2], preferred_element_type=F32) for h in heads]
        y_all = jnp.concatenate([read[pr][c:2 * c] + jnp.where(first, y_h[2 * pr], y_h[2 * pr + 1])
                                 for pr in pairs], axis=1)
        for pr in pairs:
            cols = pair_cols(pr)
            upd = lax.dot_general(jnp.concatenate([v_bf[:, cols], u_p[pr].astype(BF16)], axis=0),
                                  jnp.concatenate([k_tail[:, cols], b_tail[:, cols]], axis=0),
                                  (((0,), (0,)), ((), ())), preferred_element_type=F32)
            state_ref[pr] = states[pr] * e_last[:, cols] + jnp.where(same_head, upd, 0.0)

        mean = head_sum(y_all) * (1.0 / RWKV_DIM)
        yc = y_all - mean
        var = head_sum(yc * yc) * (1.0 / RWKV_DIM)
        yn = yc * lax.rsqrt(var + RWKV_LNX_EPS) * lnw_ref[...] + lnb_ref[...]
        yn = yn + head_sum(r_all * k_all * rk_ref[...]) * v_all
        o_ref[r, :] = (yn * gate_ref[r, :]).astype(o_ref.dtype)
        return carry

    lax.fori_loop(0, nchunks, chunk, 0)


def _rwkv7(proj, mu, w_up, w0, a_up, a0, g_up, k_k, k_a, r_k, lnx_w, lnx_b, *, batch):
    row = lambda a: a.reshape(1, -1)
    params = (row(mu), w_up.astype(BF16), row(w0), a_up.astype(BF16), row(a0), g_up.astype(BF16),
              row(k_k), row(k_a), row(r_k), row(lnx_w), row(lnx_b))
    return _mixer_call(
        _rwkv_kernel, proj, params, lambda tb, nt: [_const_spec(a) for a in params], RWKV_WIDTH,
        lambda tb: [pltpu.VMEM((tb + HALO, RWKV_IN), F32), pltpu.VMEM((tb, 3 * RWKV_WIDTH), F32),
                    pltpu.VMEM((tb, RWKV_WIDTH), F32), pltpu.VMEM((tb, RWKV_WIDTH), F32),
                    pltpu.VMEM((tb, RWKV_WIDTH), F32), pltpu.VMEM((RWKV_PAIRS, LANES, LANES), F32)],
        batch=batch, name="rwkv7")


def kernel(x, mem, positions, w_in, gdn_conv, gdn_a_log, gdn_dt_bias, gdn_norm, rwkv_mu, rwkv_w_up, rwkv_w0, rwkv_a_up, rwkv_a0, rwkv_g_up, rwkv_k_k, rwkv_k_a, rwkv_r_k, rwkv_lnx_w, rwkv_lnx_b, w_out, ln1_g, ln1_b, xattn_q, xattn_k, xattn_v, xattn_o, ln2_g, ln2_b, ffn_gate_up, ffn_down, ln3_g, ln3_b):
    batch, seq, d = x.shape
    m = batch * seq
    tm = min(1024, m)
    tm_ln = min(512, m)
    xf = x.reshape(m, d)
    xb = xf.astype(BF16)
    memb = mem.reshape(batch * N_MEM, d).astype(BF16)
    pos = positions.astype(F32).reshape(m, 1)
    w_gdn, w_ret, w_rwkv = _regroup_w_in(w_in, tk=256)

    for l in range(DEPTH):
        p_gdn = _matmul(xb, w_gdn[l], tm=tm, tn=GDN_COLS // 2, out_dtype=F32)
        p_ret = _matmul(xb, w_ret[l], tm=tm, tn=4 * RET_WIDTH // 2, out_dtype=F32)
        p_rwkv = _matmul(xb, w_rwkv[l], tm=tm, tn=RWKV_IN // 2, out_dtype=F32)
        y_a = _gated_deltanet(p_gdn, gdn_conv[l], gdn_a_log[l], gdn_dt_bias[l], gdn_norm[l], batch=batch)
        y_b = _retention(p_ret, pos, batch=batch)
        y_c = _rwkv7(p_rwkv, rwkv_mu[l], rwkv_w_up[l], rwkv_w0[l], rwkv_a_up[l], rwkv_a0[l], rwkv_g_up[l],
                     rwkv_k_k[l], rwkv_k_a[l], rwkv_r_k[l], rwkv_lnx_w[l], rwkv_lnx_b[l], batch=batch)
        xf, xb = _matmul_residual_ln([y_a, y_b, y_c], w_out[l].astype(BF16), xf, ln1_g[l], ln1_b[l],
                                     tm=tm_ln, tk=d)

        q = _matmul(xb, xattn_q[l].astype(BF16), tm=tm, tn=d // 2, out_dtype=BF16)
        kmem = _matmul(memb, xattn_k[l].astype(BF16), tm=tm, tn=d // 2, out_dtype=BF16)
        vmem = _matmul(memb, xattn_v[l].astype(BF16), tm=tm, tn=d // 2, out_dtype=BF16)
        att = _xattn_core(q, kmem, vmem, batch=batch, tq=min(512, seq))
        xf, xb = _matmul_residual_ln([att], xattn_o[l].astype(BF16), xf, ln2_g[l], ln2_b[l], tm=tm_ln, tk=d)

        hidden = _swiglu_up(xb, ffn_gate_up[l].astype(BF16), tm=tm, tn=512)
        xf, xb = _matmul_residual_ln([hidden], ffn_down[l].astype(BF16), xf, ln3_g[l], ln3_b[l],
                                     tm=tm_ln, tk=ffn_down.shape[1] // FFN_DOWN_K_STEPS)

    return xf.reshape(batch, seq, d)
```

```python
import functools
import math

import jax
import jax.numpy as jnp
from jax import lax
from jax.experimental import pallas as pl
from jax.experimental.pallas import tpu as pltpu

F32 = jnp.float32
BF16 = jnp.bfloat16

D_MODEL = 2048
DEPTH = 2
N_MEM = 256
XATTN_HEADS = 4
XATTN_DIM = D_MODEL // XATTN_HEADS

GDN_DIM = 128
GDN_HEADS = 6
GDN_WIDTH = GDN_HEADS * GDN_DIM
CONV_K = 4

RET_DIM = 128
RET_HEADS = 4
RET_WIDTH = RET_HEADS * RET_DIM
ROPE_BASE = 10000.0

RWKV_DIM = 64
RWKV_HEADS = 12
RWKV_WIDTH = RWKV_HEADS * RWKV_DIM
RWKV_PAIRS = RWKV_HEADS // 2
W_LORA = 64
A_LORA = 64
G_LORA = 128
RWKV_LNX_EPS = 64e-5
RWKV_IN = 3 * RWKV_WIDTH + W_LORA + A_LORA + G_LORA

IN_WIDTH = 4 * GDN_WIDTH + 2 * GDN_HEADS + 4 * RET_WIDTH + RWKV_IN
CHUNK = 64
MIXER_CHUNKS = 4
LOCKSTEP = 4
HALO = 8
LANES = 128
DEEPNORM_ALPHA = (2 * DEPTH) ** 0.25
FFN_DOWN_K_STEPS = 4
LN_SUB_ROWS = 256

GDN_COLS = 4 * GDN_WIDTH + 2 * LANES
VMEM_LIMIT = 56 * 1024 * 1024


def _cparams(*sem):
    return pltpu.CompilerParams(dimension_semantics=sem, vmem_limit_bytes=VMEM_LIMIT)


def _bdot(a, b):
    return jnp.dot(a.astype(BF16), b.astype(BF16), preferred_element_type=F32)


def _bdot_nt(a, b):
    return lax.dot_general(a.astype(BF16), b.astype(BF16), (((1,), (1,)), ((), ())),
                           preferred_element_type=F32)


def _bdot_tn(a, b):
    return lax.dot_general(a.astype(BF16), b.astype(BF16), (((0,), (0,)), ((), ())),
                           preferred_element_type=F32)


def _cumsum_rows(tril01, a):
    hi = a.astype(BF16)
    lo = (a - hi.astype(F32)).astype(BF16)
    return jnp.dot(tril01, hi, preferred_element_type=F32) + jnp.dot(tril01, lo, preferred_element_type=F32)


def _unit_lower_inverses(ws):
    c = ws[0].shape[0]
    right = lax.broadcasted_iota(jnp.int32, (c, 2 * c), 1) >= c
    for _ in range(int(math.log2(c))):
        prods = [_bdot(w[:, 0:c], w) for w in ws]
        ws = [prod + jnp.where(right, w, 0.0) for prod, w in zip(prods, ws)]
    return ws


def _tri_masks(c):
    rows = lax.broadcasted_iota(jnp.int32, (c, c), 0)
    cols = lax.broadcasted_iota(jnp.int32, (c, c), 1)
    return rows >= cols, rows > cols


def _sigmoid(x):
    return 1.0 / (1.0 + jnp.exp(-x))


def _silu(x):
    return x * _sigmoid(x)


def _softplus(x):
    return jnp.maximum(x, 0.0) + jnp.log1p(jnp.exp(-jnp.abs(x)))


def _chunk_rows(i):
    return pl.ds(pl.multiple_of(i * CHUNK, CHUNK), CHUNK)


def _regroup_kernel(w_ref, gdn_ref, ret_ref, rwkv_ref):
    w = w_ref[...]
    o_beta = 4 * GDN_WIDTH
    o_alpha = o_beta + GDN_HEADS
    o_ret = o_alpha + GDN_HEADS
    o_rwkv = o_ret + 4 * RET_WIDTH
    lane = lax.broadcasted_iota(jnp.int32, (w.shape[0], LANES), 1)
    gdn_ref[:, 0:o_beta] = w[:, 0:o_beta].astype(BF16)
    gdn_ref[:, o_beta:o_beta + LANES] = jnp.where(lane < GDN_HEADS, w[:, o_beta:o_beta + LANES], 0.0).astype(BF16)
    gdn_ref[:, o_beta + LANES:GDN_COLS] = jnp.where(lane < GDN_HEADS, w[:, o_alpha:o_alpha + LANES], 0.0).astype(BF16)
    ret_ref[...] = w[:, o_ret:o_rwkv].astype(BF16)
    rwkv_ref[...] = w[:, o_rwkv:IN_WIDTH].astype(BF16)


def _regroup_w_in(w_in, *, tk):
    depth, k, n = w_in.shape
    assert n == IN_WIDTH and k % tk == 0
    widths = (GDN_COLS, 4 * RET_WIDTH, RWKV_IN)
    return pl.pallas_call(
        _regroup_kernel,
        grid=(depth, k // tk),
        in_specs=[pl.BlockSpec((None, tk, n), lambda l, i: (l, i, 0))],
        out_specs=[pl.BlockSpec((None, tk, wd), lambda l, i: (l, i, 0)) for wd in widths],
        out_shape=[jax.ShapeDtypeStruct((depth, k, wd), BF16) for wd in widths],
        compiler_params=_cparams("parallel", "arbitrary"),
        name="regroup_w_in",
    )(w_in)


def _mm_kernel(x_ref, w_ref, o_ref):
    o_ref[...] = jnp.dot(x_ref[...].astype(BF16), w_ref[...], preferred_element_type=F32).astype(o_ref.dtype)


def _matmul(x, w, *, tm, tn, out_dtype):
    m, k = x.shape
    n = w.shape[1]
    assert m % tm == 0 and n % tn == 0
    return pl.pallas_call(
        _mm_kernel,
        grid=(m // tm, n // tn),
        in_specs=[pl.BlockSpec((tm, k), lambda i, j: (i, 0)),
                  pl.BlockSpec((k, tn), lambda i, j: (0, j))],
        out_specs=pl.BlockSpec((tm, tn), lambda i, j: (i, j)),
        out_shape=jax.ShapeDtypeStruct((m, n), out_dtype),
        compiler_params=_cparams("parallel", "arbitrary"),
        name="proj_matmul",
    )(x, w)


def _mm_ln_kernel(*refs, n_x, nk, sub):
    x_refs = refs[:n_x]
    w_ref, res_ref, g_ref, b_ref, of_ref, ob_ref, acc_ref = refs[n_x:]
    k = pl.program_id(1)

    def partial_product(rows):
        total, row = None, 0
        for x_ref in x_refs:
            width = x_ref.shape[1]
            part = jnp.dot(x_ref[rows, :], w_ref[row:row + width, :], preferred_element_type=F32)
            total = part if total is None else total + part
            row += width
        return total

    if nk > 1:
        @pl.when(k == 0)
        def _():
            acc_ref[...] = partial_product(slice(None))

        @pl.when(jnp.logical_and(k > 0, k < nk - 1))
        def _():
            acc_ref[...] += partial_product(slice(None))

    @pl.when(k == nk - 1)
    def _():
        for s in range(res_ref.shape[0] // sub):
            rows = slice(s * sub, (s + 1) * sub)
            acc = partial_product(rows)
            if nk > 1:
                acc = acc + acc_ref[rows, :]
            y = DEEPNORM_ALPHA * res_ref[rows, :] + acc
            mu = jnp.mean(y, axis=-1, keepdims=True)
            yc = y - mu
            var = jnp.mean(yc * yc, axis=-1, keepdims=True)
            out = yc * lax.rsqrt(var + 1e-5) * g_ref[...] + b_ref[...]
            of_ref[rows, :] = out
            ob_ref[rows, :] = out.astype(BF16)


def _matmul_residual_ln(xs, w, res, g, b, *, tm, tk):
    m = res.shape[0]
    kdim, n = w.shape
    assert m % tm == 0 and kdim % tk == 0 and sum(x.shape[1] for x in xs) == kdim
    nk = kdim // tk
    assert len(xs) == 1 or nk == 1
    x_specs = [pl.BlockSpec((tm, tk if len(xs) == 1 else x.shape[1]), lambda i, k: (i, k)) for x in xs]
    return pl.pallas_call(
        functools.partial(_mm_ln_kernel, n_x=len(xs), nk=nk, sub=min(LN_SUB_ROWS, tm)),
        grid=(m // tm, nk),
        in_specs=x_specs + [pl.BlockSpec((tk, n), lambda i, k: (k, 0)),
                            pl.BlockSpec((tm, n), lambda i, k: (i, 0)),
                            pl.BlockSpec((1, n), lambda i, k: (0, 0)),
                            pl.BlockSpec((1, n), lambda i, k: (0, 0))],
        out_specs=[pl.BlockSpec((tm, n), lambda i, k: (i, 0)),
                   pl.BlockSpec((tm, n), lambda i, k: (i, 0))],
        out_shape=[jax.ShapeDtypeStruct((m, n), F32), jax.ShapeDtypeStruct((m, n), BF16)],
        scratch_shapes=[pltpu.VMEM((tm, n), F32)],
        compiler_params=_cparams("parallel", "arbitrary"),
        name="proj_residual_layernorm",
    )(*xs, w, res, g.reshape(1, n), b.reshape(1, n))


def _swiglu_kernel(x_ref, wg_ref, wu_ref, o_ref):
    x = x_ref[...]
    gate = jnp.dot(x, wg_ref[...], preferred_element_type=F32)
    up = jnp.dot(x, wu_ref[...], preferred_element_type=F32)
    o_ref[...] = (_silu(gate) * up).astype(o_ref.dtype)


def _swiglu_up(x, w_gate_up, *, tm, tn):
    m, k = x.shape
    h = w_gate_up.shape[1] // 2
    assert m % tm == 0 and h % tn == 0
    nj = h // tn
    return pl.pallas_call(
        _swiglu_kernel,
        grid=(m // tm, nj),
        in_specs=[pl.BlockSpec((tm, k), lambda i, j: (i, 0)),
                  pl.BlockSpec((k, tn), lambda i, j: (0, j)),
                  pl.BlockSpec((k, tn), lambda i, j: (0, j + nj))],
        out_specs=pl.BlockSpec((tm, tn), lambda i, j: (i, j)),
        out_shape=jax.ShapeDtypeStruct((m, h), BF16),
        compiler_params=_cparams("parallel", "arbitrary"),
        name="swiglu_up",
    )(x, w_gate_up, w_gate_up)


def _xattn_kernel(q_ref, k_ref, v_ref, o_ref):
    scale = XATTN_DIM ** -0.5
    for h in range(XATTN_HEADS):
        sl = slice(h * XATTN_DIM, (h + 1) * XATTN_DIM)
        s = lax.dot_general(q_ref[:, sl], k_ref[:, sl], (((1,), (1,)), ((), ())),
                            preferred_element_type=F32) * scale
        s = s - jnp.max(s, axis=-1, keepdims=True)
        p = jnp.exp(s)
        p = p / jnp.sum(p, axis=-1, keepdims=True)
        o_ref[:, sl] = jnp.dot(p.astype(BF16), v_ref[:, sl], preferred_element_type=F32).astype(o_ref.dtype)


def _xattn_core(q, k, v, *, batch, tq):
    m = q.shape[0]
    t = m // batch
    assert t % tq == 0
    nt = t // tq
    return pl.pallas_call(
        _xattn_kernel,
        grid=(batch, nt),
        in_specs=[pl.BlockSpec((tq, D_MODEL), lambda b, i: (b * nt + i, 0)),
                  pl.BlockSpec((N_MEM, D_MODEL), lambda b, i: (b, 0)),
                  pl.BlockSpec((N_MEM, D_MODEL), lambda b, i: (b, 0))],
        out_specs=pl.BlockSpec((tq, D_MODEL), lambda b, i: (b * nt + i, 0)),
        out_shape=jax.ShapeDtypeStruct((m, D_MODEL), BF16),
        compiler_params=_cparams("parallel", "arbitrary"),
        name="xattn_core",
    )(q, k, v)


def _mixer_call(kernel_fn, proj, extra, extra_specs, out_width, scratch, *, batch, name):
    m, cols = proj.shape
    tb = min(MIXER_CHUNKS * CHUNK, m // batch)
    nt = m // batch // tb
    assert m == batch * nt * tb and tb % CHUNK == 0
    return pl.pallas_call(
        functools.partial(kernel_fn, nchunks=tb // CHUNK),
        grid=(batch, nt),
        in_specs=[pl.BlockSpec((tb, cols), lambda b, i: (b * nt + i, 0))] + extra_specs(tb, nt),
        out_specs=pl.BlockSpec((tb, out_width), lambda b, i: (b * nt + i, 0)),
        out_shape=jax.ShapeDtypeStruct((m, out_width), BF16),
        scratch_shapes=scratch(tb),
        compiler_params=_cparams("parallel", "arbitrary"),
        name=name,
    )(proj, *extra)


def _const_spec(a):
    return pl.BlockSpec(a.shape, lambda b, i: (0,) * a.ndim)


def _retention_kernel(p_ref, pos_ref, freq_ref, sign_ref, o_ref, state_ref, *, nchunks):
    c = CHUNK

    @pl.when(pl.program_id(1) == 0)
    def _():
        state_ref[...] = jnp.zeros_like(state_ref)

    heads = range(RET_HEADS)
    head_cols = lambda group, h: slice(group * RET_WIDTH + h * RET_DIM, group * RET_WIDTH + (h + 1) * RET_DIM)
    rows = lax.broadcasted_iota(jnp.int32, (c, c), 0)
    cols = lax.broadcasted_iota(jnp.int32, (c, c), 1)
    rel = (rows - cols).astype(F32)
    tpos = lax.broadcasted_iota(jnp.int32, (c, RET_DIM), 0).astype(F32)
    log_gammas = [math.log1p(-2.0 ** (-5.0 - h)) for h in heads]
    decay_masks = [jnp.where(rel >= 0, jnp.exp(jnp.where(rel >= 0, rel, 0.0) * lg), 0.0) for lg in log_gammas]
    q_decays = [jnp.exp((tpos + 1.0) * lg) for lg in log_gammas]
    k_decays = [jnp.exp((c - 1.0 - tpos) * lg) for lg in log_gammas]

    def chunk(i, carry):
        r = _chunk_rows(i)
        ang = pos_ref[r, :] * freq_ref[...]
        cos = jnp.cos(ang)
        sin_signed = jnp.sin(ang) * sign_ref[...]

        def rotary(y):
            return y * cos + pltpu.roll(y, RET_DIM // 2, 1) * sin_signed

        qs = [rotary(p_ref[r, head_cols(0, h)]) for h in heads]
        ks = [rotary(p_ref[r, head_cols(1, h)]) * RET_DIM ** -0.5 for h in heads]
        vs = [p_ref[r, head_cols(2, h)].astype(BF16) for h in heads]
        scores = [_bdot_nt(qs[h], ks[h]) * decay_masks[h] for h in heads]
        states = [state_ref[h] for h in heads]
        ys = [_bdot(scores[h], vs[h]) + _bdot(qs[h] * q_decays[h], states[h]) for h in heads]
        for h in heads:
            state_ref[h] = states[h] * math.exp(c * log_gammas[h]) + _bdot_tn(ks[h] * k_decays[h], vs[h])
        for h in heads:
            y = ys[h]
            mu = jnp.mean(y, axis=-1, keepdims=True)
            yc = y - mu
            var = jnp.mean(yc * yc, axis=-1, keepdims=True)
            o_ref[r, head_cols(0, h)] = (_silu(p_ref[r, head_cols(3, h)])
                                         * (yc * lax.rsqrt(var + 1e-6))).astype(o_ref.dtype)
        return carry

    lax.fori_loop(0, nchunks, chunk, 0)


def _retention(proj, pos, *, batch):
    d = RET_DIM
    inv_freq = ROPE_BASE ** (-jnp.arange(0, d, 2, dtype=F32) / d)
    freq2 = jnp.concatenate([inv_freq, inv_freq]).reshape(1, d)
    sign = jnp.concatenate([-jnp.ones((d // 2,), F32), jnp.ones((d // 2,), F32)]).reshape(1, d)
    return _mixer_call(
        _retention_kernel, proj, (pos, freq2, sign),
        lambda tb, nt: [pl.BlockSpec((tb, 1), lambda b, i: (b * nt + i, 0)), _const_spec(freq2), _const_spec(sign)],
        RET_WIDTH, lambda tb: [pltpu.VMEM((RET_HEADS, d, d), F32)], batch=batch, name="retention")


def _gdn_kernel(p_ref, conv_ref, alog_ref, dtb_ref, nw_ref, o_ref, xpad_ref, qkv_ref, state_ref, *, nchunks):
    c = CHUNK
    tb = nchunks * c
    w3 = 3 * GDN_WIDTH

    @pl.when(pl.program_id(1) == 0)
    def _():
        state_ref[...] = jnp.zeros_like(state_ref)
        xpad_ref[0:HALO, :] = jnp.zeros((HALO, w3), F32)

    @pl.when(pl.program_id(1) != 0)
    def _():
        xpad_ref[0:HALO, :] = xpad_ref[tb:tb + HALO, :]

    xpad_ref[HALO:HALO + tb, :] = p_ref[:, 0:w3]
    xpad = xpad_ref[...]
    conv = conv_ref[CONV_K - 1:CONV_K, :] * xpad[HALO:HALO + tb]
    for back in range(1, CONV_K):
        shifted = pltpu.roll(xpad, back, 0)[HALO:HALO + tb]
        conv = conv + conv_ref[CONV_K - 1 - back:CONV_K - back, :] * shifted
    qkv_ref[...] = _silu(conv)

    causal, strict = _tri_masks(c)
    tril01 = jnp.where(causal, 1.0, 0.0).astype(BF16)
    eye = jnp.where(lax.broadcasted_iota(jnp.int32, (c, c), 0) == lax.broadcasted_iota(jnp.int32, (c, c), 1),
                    1.0, 0.0)
    right = lax.broadcasted_iota(jnp.int32, (c, 2 * c), 1) >= c
    heads = range(GDN_HEADS)
    head_cols = lambda group, h: slice(group * GDN_WIDTH + h * GDN_DIM, group * GDN_WIDTH + (h + 1) * GDN_DIM)

    def state_free(r):
        beta_all = _sigmoid(p_ref[r, 4 * GDN_WIDTH:4 * GDN_WIDTH + LANES])
        log_a = -jnp.exp(alog_ref[...]) * _softplus(p_ref[r, 4 * GDN_WIDTH + LANES:GDN_COLS] + dtb_ref[...])
        cum_all = _cumsum_rows(tril01, log_a)
        cum_rows = cum_all.T
        qs = [qkv_ref[r, head_cols(0, h)] for h in heads]
        ks = [qkv_ref[r, head_cols(1, h)] for h in heads]
        vs = [qkv_ref[r, head_cols(2, h)] for h in heads]
        qs = [q * lax.rsqrt(jnp.sum(q * q, axis=-1, keepdims=True) + 1e-6) * GDN_DIM ** -0.5 for q in qs]
        ks = [k * lax.rsqrt(jnp.sum(k * k, axis=-1, keepdims=True) + 1e-6) for k in ks]
        betas = [beta_all[:, h:h + 1] for h in heads]
        cums = [cum_all[:, h:h + 1] for h in heads]
        lasts = [cum_all[c - 1:c, h:h + 1] for h in heads]
        segs = [jnp.where(causal, jnp.exp(jnp.where(causal, cums[h] - cum_rows[h:h + 1, :], 0.0)), 0.0)
                for h in heads]
        kbs = [ks[h] * betas[h] for h in heads]
        ecums = [jnp.exp(cums[h]) for h in heads]
        scores = [_bdot_nt(jnp.concatenate([kbs[h], qs[h]], axis=0), ks[h]) for h in heads]
        return dict(
            r=r, qs=qs, ks=ks, ecums=ecums, cums=cums, lasts=lasts,
            inv_in=[jnp.concatenate([jnp.where(strict, -scores[h][0:c] * segs[h], 0.0), eye], axis=1)
                    for h in heads],
            qks=[scores[h][c:2 * c] * segs[h] for h in heads],
            rhs=[jnp.concatenate([vs[h] * betas[h], kbs[h] * ecums[h]], axis=1).astype(BF16) for h in heads])

    def chunk_group(j, carry):
        parts = [state_free(_chunk_rows(LOCKSTEP * j + ci)) for ci in range(LOCKSTEP)]
        t_invs = _unit_lower_inverses([w for part in parts for w in part["inv_in"]])
        states = [state_ref[h] for h in heads]
        for ci, part in enumerate(parts):
            uws = [_bdot(jnp.where(right, t_invs[ci * GDN_HEADS + h], 0.0),
                         jnp.concatenate([part["rhs"][h], part["rhs"][h]], axis=0)) for h in heads]
            wq_s = [_bdot(jnp.concatenate([uws[h][:, GDN_DIM:2 * GDN_DIM], part["qs"][h] * part["ecums"][h]], axis=0),
                          states[h]) for h in heads]
            v_news = [uws[h][:, 0:GDN_DIM] - wq_s[h][0:c] for h in heads]
            outs = [wq_s[h][c:2 * c] + _bdot(part["qks"][h], v_news[h]) for h in heads]
            states = [states[h] * jnp.exp(part["lasts"][h])
                      + _bdot_tn(part["ks"][h] * jnp.exp(part["lasts"][h] - part["cums"][h]), v_news[h])
                      for h in heads]
            for h in heads:
                o = outs[h]
                o = o * lax.rsqrt(jnp.mean(o * o, axis=-1, keepdims=True) + 1e-6) * nw_ref[...]
                o_ref[part["r"], head_cols(0, h)] = (o * _silu(p_ref[part["r"], head_cols(3, h)])).astype(o_ref.dtype)
        for h in heads:
            state_ref[h] = states[h]
        return carry

    assert nchunks % LOCKSTEP == 0
    lax.fori_loop(0, nchunks // LOCKSTEP, chunk_group, 0)


def _gated_deltanet(proj, conv_w, a_log, dt_bias, norm_w, *, batch):
    pad = lambda a: jnp.pad(a.reshape(1, -1), ((0, 0), (0, LANES - a.shape[-1])))
    extra = (conv_w, pad(a_log), pad(dt_bias), norm_w.reshape(1, GDN_DIM))
    return _mixer_call(
        _gdn_kernel, proj, extra, lambda tb, nt: [_const_spec(a) for a in extra], GDN_WIDTH,
        lambda tb: [pltpu.VMEM((tb + HALO, 3 * GDN_WIDTH), F32), pltpu.VMEM((tb, 3 * GDN_WIDTH), F32),
                    pltpu.VMEM((GDN_HEADS, GDN_DIM, GDN_DIM), F32)],
        batch=batch, name="gated_deltanet")


def _rwkv_kernel(p_ref, mu_ref, wup_ref, w0_ref, aup_ref, a0_ref, gup_ref, kk_ref, ka_ref, rk_ref,
                 lnw_ref, lnb_ref, o_ref, prev_ref, rkv_ref, logw_ref, asig_ref, gate_ref, state_ref, *, nchunks):
    c = CHUNK
    tb = nchunks * c
    wd_ = RWKV_WIDTH

    @pl.when(pl.program_id(1) == 0)
    def _():
        state_ref[...] = jnp.zeros_like(state_ref)
        prev_ref[0:HALO, :] = jnp.zeros((HALO, RWKV_IN), F32)

    @pl.when(pl.program_id(1) != 0)
    def _():
        prev_ref[0:HALO, :] = prev_ref[tb:tb + HALO, :]

    prev_ref[HALO:HALO + tb, :] = p_ref[...]
    p = p_ref[...]
    p = p + (pltpu.roll(prev_ref[...], 1, 0)[HALO:HALO + tb] - p) * mu_ref[...]
    rkv_ref[...] = p[:, 0:3 * wd_]
    w_down = p[:, 3 * wd_:3 * wd_ + W_LORA]
    a_down = p[:, 3 * wd_ + W_LORA:3 * wd_ + W_LORA + A_LORA]
    g_down = p[:, 3 * wd_ + W_LORA + A_LORA:RWKV_IN]
    w_log = -_softplus(-(w0_ref[...] + _bdot(jnp.tanh(w_down), wup_ref[...]))) - 0.5
    logw_ref[...] = -jnp.exp(w_log)
    asig_ref[...] = _sigmoid(a0_ref[...] + _bdot(a_down, aup_ref[...]))
    gate_ref[...] = _bdot(_sigmoid(g_down), gup_ref[...])

    causal, strict = _tri_masks(c)
    tril01 = jnp.where(causal, 1.0, 0.0).astype(BF16)
    lane = lax.broadcasted_iota(jnp.int32, (c, LANES), 1)
    first = lane < RWKV_DIM
    rl = lax.broadcasted_iota(jnp.int32, (LANES, LANES), 0) < RWKV_DIM
    cl = lax.broadcasted_iota(jnp.int32, (LANES, LANES), 1) < RWKV_DIM
    same_head = rl == cl
    head_ones = jnp.where(same_head, 1.0, 0.0).astype(BF16)
    row2 = lax.broadcasted_iota(jnp.int32, (c, 2 * c), 0)
    col2 = lax.broadcasted_iota(jnp.int32, (c, 2 * c), 1)
    causal2 = row2 >= (col2 & (c - 1))
    right = col2 >= c
    strict_lo = row2 > col2
    strict_hi = jnp.logical_and(right, row2 + c > col2)
    eye_hi = jnp.where(col2 == row2 + c, 1.0, 0.0)
    pairs = range(RWKV_PAIRS)
    heads = range(RWKV_HEADS)
    pair_cols = lambda pr: slice(pr * LANES, (pr + 1) * LANES)
    own = [first if h % 2 == 0 else jnp.logical_not(first) for h in heads]
    zero = jnp.zeros((), BF16)

    def head_sum(y):
        return jnp.concatenate([_bdot(y[:, pair_cols(pr)], head_ones) for pr in pairs], axis=1)

    def state_free(r):
        r_all = rkv_ref[r, 0:wd_]
        k_in = rkv_ref[r, wd_:2 * wd_]
        v_all = rkv_ref[r, 2 * wd_:3 * wd_]
        log_w = logw_ref[r, :]
        a_sig = asig_ref[r, :]
        cum = _cumsum_rows(tril01, log_w)

        kk = k_in * kk_ref[...]
        kk = kk * lax.rsqrt(head_sum(kk * kk) + 1e-12)
        k_all = k_in * (1.0 + (a_sig - 1.0) * ka_ref[...])
        a_vec = -kk
        b_vec = kk * a_sig
        ref_c = cum[c // 2 - 1:c // 2, :]
        last = cum[c - 1:c, :]
        e_out = jnp.exp(ref_c - cum)
        e_tail = jnp.exp(last - cum)
        a_in = (a_vec * jnp.exp(cum - log_w - ref_c)).astype(BF16)
        r_in = (r_all * jnp.exp(cum - ref_c)).astype(BF16)
        a_prev = (a_vec * jnp.exp(cum - log_w)).astype(BF16)
        r_cum = (r_all * jnp.exp(cum)).astype(BF16)
        b_out = (b_vec * e_out).astype(BF16)
        k_out = (k_all * e_out).astype(BF16)
        k_tail = (k_all * e_tail).astype(BF16)
        b_tail = (b_vec * e_tail).astype(BF16)
        v_bf = v_all.astype(BF16)
        e_last = jnp.exp(last)

        outs_p = [jnp.concatenate([b_out[:, pair_cols(pr)], k_out[:, pair_cols(pr)]], axis=0) for pr in pairs]
        scores = []
        for h in heads:
            cols = pair_cols(h // 2)
            x = jnp.concatenate([jnp.where(own[h], a_in[:, cols], zero), jnp.where(own[h], r_in[:, cols], zero)],
                                axis=0)
            scores.append(lax.dot_general(x, outs_p[h // 2], (((1,), (1,)), ((), ())), preferred_element_type=F32))
        vv = [jnp.concatenate([v_bf[:, pair_cols(pr)], v_bf[:, pair_cols(pr)]], axis=0) for pr in pairs]
        ak_hi = [jnp.where(strict_hi, scores[h][0:c, :], 0.0).astype(BF16) for h in heads]
        return dict(
            r=r, v_bf=v_bf, v_all=v_all, e_last=e_last,
            inv_in=[jnp.where(strict_lo, scores[h][0:c, :], eye_hi) for h in heads],
            rbk=[jnp.where(causal2, scores[h][c:2 * c, :], 0.0).astype(BF16) for h in heads],
            akv=[jnp.dot(ak_hi[h], vv[h // 2], preferred_element_type=F32) for h in heads],
            reads=[jnp.concatenate([a_prev[:, pair_cols(pr)], r_cum[:, pair_cols(pr)]], axis=0) for pr in pairs],
            tails=[jnp.concatenate([k_tail[:, pair_cols(pr)], b_tail[:, pair_cols(pr)]], axis=0) for pr in pairs],
            bonus=head_sum(r_all * k_all * rk_ref[...]) * v_all)

    def chunk_group(j, carry):
        parts = [state_free(_chunk_rows(LOCKSTEP * j + ci)) for ci in range(LOCKSTEP)]
        t_invs = _unit_lower_inverses([w for part in parts for w in part["inv_in"]])
        states = [state_ref[pr] for pr in pairs]
        for ci, part in enumerate(parts):
            v_bf = part["v_bf"]
            read = [lax.dot_general(part["reads"][pr], states[pr].astype(BF16), (((1,), (1,)), ((), ())),
                                    preferred_element_type=F32) for pr in pairs]
            rhs = [read[pr][0:c] + jnp.where(first, part["akv"][2 * pr], part["akv"][2 * pr + 1]) for pr in pairs]
            rhs2 = [jnp.concatenate([rhs[pr].astype(BF16)] * 2, axis=0) for pr in pairs]
            u_h = [_bdot(jnp.where(right, t_invs[ci * RWKV_HEADS + h], 0.0), rhs2[h // 2]) for h in heads]
            u_p = [jnp.where(first, u_h[2 * pr], u_h[2 * pr + 1]) for pr in pairs]
            uv = [jnp.concatenate([u_p[pr].astype(BF16), v_bf[:, pair_cols(pr)]], axis=0) for pr in pairs]
            y_h = [jnp.dot(part["rbk"][h], uv[h // 2], preferred_element_type=F32) for h in heads]
            y_all = jnp.concatenate([read[pr][c:2 * c] + jnp.where(first, y_h[2 * pr], y_h[2 * pr + 1])
                                     for pr in pairs], axis=1)
            new_states = []
            for pr in pairs:
                cols = pair_cols(pr)
                upd = lax.dot_general(jnp.concatenate([v_bf[:, cols], u_p[pr].astype(BF16)], axis=0),
                                      part["tails"][pr], (((0,), (0,)), ((), ())), preferred_element_type=F32)
                new_states.append(states[pr] * part["e_last"][:, cols] + jnp.where(same_head, upd, 0.0))
            states = new_states

            mean = head_sum(y_all) * (1.0 / RWKV_DIM)
            yc = y_all - mean
            var = head_sum(yc * yc) * (1.0 / RWKV_DIM)
            yn = yc * lax.rsqrt(var + RWKV_LNX_EPS) * lnw_ref[...] + lnb_ref[...] + part["bonus"]
            o_ref[part["r"], :] = (yn * gate_ref[part["r"], :]).astype(o_ref.dtype)
        for pr in pairs:
            state_ref[pr] = states[pr]
        return carry

    assert nchunks % LOCKSTEP == 0
    lax.fori_loop(0, nchunks // LOCKSTEP, chunk_group, 0)


def _rwkv7(proj, mu, w_up, w0, a_up, a0, g_up, k_k, k_a, r_k, lnx_w, lnx_b, *, batch):
    row = lambda a: a.reshape(1, -1)
    params = (row(mu), w_up.astype(BF16), row(w0), a_up.astype(BF16), row(a0), g_up.astype(BF16),
              row(k_k), row(k_a), row(r_k), row(lnx_w), row(lnx_b))
    return _mixer_call(
        _rwkv_kernel, proj, params, lambda tb, nt: [_const_spec(a) for a in params], RWKV_WIDTH,
        lambda tb: [pltpu.VMEM((tb + HALO, RWKV_IN), F32), pltpu.VMEM((tb, 3 * RWKV_WIDTH), F32),
                    pltpu.VMEM((tb, RWKV_WIDTH), F32), pltpu.VMEM((tb, RWKV_WIDTH), F32),
                    pltpu.VMEM((tb, RWKV_WIDTH), F32), pltpu.VMEM((RWKV_PAIRS, LANES, LANES), F32)],
        batch=batch, name="rwkv7")


def kernel(x, mem, positions, w_in, gdn_conv, gdn_a_log, gdn_dt_bias, gdn_norm, rwkv_mu, rwkv_w_up, rwkv_w0, rwkv_a_up, rwkv_a0, rwkv_g_up, rwkv_k_k, rwkv_k_a, rwkv_r_k, rwkv_lnx_w, rwkv_lnx_b, w_out, ln1_g, ln1_b, xattn_q, xattn_k, xattn_v, xattn_o, ln2_g, ln2_b, ffn_gate_up, ffn_down, ln3_g, ln3_b):
    batch, seq, d = x.shape
    m = batch * seq
    tm = min(1024, m)
    tm_ln = min(512, m)
    xf = x.reshape(m, d)
    xb = xf.astype(BF16)
    memb = mem.reshape(batch * N_MEM, d).astype(BF16)
    pos = positions.astype(F32).reshape(m, 1)
    w_gdn, w_ret, w_rwkv = _regroup_w_in(w_in, tk=256)

    for l in range(DEPTH):
        p_gdn = _matmul(xb, w_gdn[l], tm=tm, tn=GDN_COLS // 2, out_dtype=F32)
        p_ret = _matmul(xb, w_ret[l], tm=tm, tn=4 * RET_WIDTH // 2, out_dtype=F32)
        p_rwkv = _matmul(xb, w_rwkv[l], tm=tm, tn=RWKV_IN // 2, out_dtype=F32)
        y_a = _gated_deltanet(p_gdn, gdn_conv[l], gdn_a_log[l], gdn_dt_bias[l], gdn_norm[l], batch=batch)
        y_b = _retention(p_ret, pos, batch=batch)
        y_c = _rwkv7(p_rwkv, rwkv_mu[l], rwkv_w_up[l], rwkv_w0[l], rwkv_a_up[l], rwkv_a0[l], rwkv_g_up[l],
                     rwkv_k_k[l], rwkv_k_a[l], rwkv_r_k[l], rwkv_lnx_w[l], rwkv_lnx_b[l], batch=batch)
        xf, xb = _matmul_residual_ln([y_a, y_b, y_c], w_out[l].astype(BF16), xf, ln1_g[l], ln1_b[l],
                                     tm=tm_ln, tk=d)

        q = _matmul(xb, xattn_q[l].astype(BF16), tm=tm, tn=d // 2, out_dtype=BF16)
        kmem = _matmul(memb, xattn_k[l].astype(BF16), tm=tm, tn=d // 2, out_dtype=BF16)
        vmem = _matmul(memb, xattn_v[l].astype(BF16), tm=tm, tn=d // 2, out_dtype=BF16)
        att = _xattn_core(q, kmem, vmem, batch=batch, tq=min(512, seq))
        xf, xb = _matmul_residual_ln([att], xattn_o[l].astype(BF16), xf, ln2_g[l], ln2_b[l], tm=tm_ln, tk=d)

        hidden = _swiglu_up(xb, ffn_gate_up[l].astype(BF16), tm=tm, tn=512)
        xf, xb = _matmul_residual_ln([hidden], ffn_down[l].astype(BF16), xf, ln3_g[l], ln3_b[l],
                                     tm=tm_ln, tk=ffn_down.shape[1] // FFN_DOWN_K_STEPS)

    return xf.reshape(batch, seq, d)
```

```python
import functools
import math

import jax
import jax.numpy as jnp
from jax import lax
from jax.experimental import pallas as pl
from jax.experimental.pallas import tpu as pltpu

F32 = jnp.float32
BF16 = jnp.bfloat16

D_MODEL = 2048
DEPTH = 2
N_MEM = 256
XATTN_HEADS = 4
XATTN_DIM = D_MODEL // XATTN_HEADS

GDN_DIM = 128
GDN_HEADS = 6
GDN_WIDTH = GDN_HEADS * GDN_DIM
CONV_K = 4

RET_DIM = 128
RET_HEADS = 4
RET_WIDTH = RET_HEADS * RET_DIM
ROPE_BASE = 10000.0

RWKV_DIM = 64
RWKV_HEADS = 12
RWKV_WIDTH = RWKV_HEADS * RWKV_DIM
RWKV_PAIRS = RWKV_HEADS // 2
W_LORA = 64
A_LORA = 64
G_LORA = 128
RWKV_LNX_EPS = 64e-5
RWKV_IN = 3 * RWKV_WIDTH + W_LORA + A_LORA + G_LORA

IN_WIDTH = 4 * GDN_WIDTH + 2 * GDN_HEADS + 4 * RET_WIDTH + RWKV_IN
CHUNK = 64
MIXER_CHUNKS = 4
LOCKSTEP = 4
HALO = 8
LANES = 128
DEEPNORM_ALPHA = (2 * DEPTH) ** 0.25
FFN_DOWN_K_STEPS = 4
LN_SUB_ROWS = 256

GDN_COLS = 4 * GDN_WIDTH + 2 * LANES
VMEM_LIMIT = 56 * 1024 * 1024


def _cparams(*sem):
    return pltpu.CompilerParams(dimension_semantics=sem, vmem_limit_bytes=VMEM_LIMIT)


def _bdot(a, b):
    return jnp.dot(a.astype(BF16), b.astype(BF16), preferred_element_type=F32)


def _bdot_nt(a, b):
    return lax.dot_general(a.astype(BF16), b.astype(BF16), (((1,), (1,)), ((), ())),
                           preferred_element_type=F32)


def _bdot_tn(a, b):
    return lax.dot_general(a.astype(BF16), b.astype(BF16), (((0,), (0,)), ((), ())),
                           preferred_element_type=F32)


def _cumsum_rows(tril01, a):
    hi = a.astype(BF16)
    lo = (a - hi.astype(F32)).astype(BF16)
    return jnp.dot(tril01, hi, preferred_element_type=F32) + jnp.dot(tril01, lo, preferred_element_type=F32)


def _unit_lower_inverses(ws):
    c = ws[0].shape[0]
    right = lax.broadcasted_iota(jnp.int32, (c, 2 * c), 1) >= c
    for _ in range(int(math.log2(c))):
        prods = [_bdot(w[:, 0:c], w) for w in ws]
        ws = [prod + jnp.where(right, w, 0.0) for prod, w in zip(prods, ws)]
    return ws


def _tri_masks(c):
    rows = lax.broadcasted_iota(jnp.int32, (c, c), 0)
    cols = lax.broadcasted_iota(jnp.int32, (c, c), 1)
    return rows >= cols, rows > cols


def _sigmoid(x):
    return 1.0 / (1.0 + jnp.exp(-x))


def _silu(x):
    return x * _sigmoid(x)


def _softplus(x):
    return jnp.maximum(x, 0.0) + jnp.log1p(jnp.exp(-jnp.abs(x)))


def _chunk_rows(i):
    return pl.ds(pl.multiple_of(i * CHUNK, CHUNK), CHUNK)


def _regroup_kernel(w_ref, gdn_ref, ret_ref, rwkv_ref):
    w = w_ref[...]
    o_beta = 4 * GDN_WIDTH
    o_alpha = o_beta + GDN_HEADS
    o_ret = o_alpha + GDN_HEADS
    o_rwkv = o_ret + 4 * RET_WIDTH
    lane = lax.broadcasted_iota(jnp.int32, (w.shape[0], LANES), 1)
    gdn_ref[:, 0:o_beta] = w[:, 0:o_beta].astype(BF16)
    gdn_ref[:, o_beta:o_beta + LANES] = jnp.where(lane < GDN_HEADS, w[:, o_beta:o_beta + LANES], 0.0).astype(BF16)
    gdn_ref[:, o_beta + LANES:GDN_COLS] = jnp.where(lane < GDN_HEADS, w[:, o_alpha:o_alpha + LANES], 0.0).astype(BF16)
    ret_ref[...] = w[:, o_ret:o_rwkv].astype(BF16)
    rwkv_ref[...] = w[:, o_rwkv:IN_WIDTH].astype(BF16)


def _regroup_w_in(w_in, *, tk):
    depth, k, n = w_in.shape
    assert n == IN_WIDTH and k % tk == 0
    widths = (GDN_COLS, 4 * RET_WIDTH, RWKV_IN)
    return pl.pallas_call(
        _regroup_kernel,
        grid=(depth, k // tk),
        in_specs=[pl.BlockSpec((None, tk, n), lambda l, i: (l, i, 0))],
        out_specs=[pl.BlockSpec((None, tk, wd), lambda l, i: (l, i, 0)) for wd in widths],
        out_shape=[jax.ShapeDtypeStruct((depth, k, wd), BF16) for wd in widths],
        compiler_params=_cparams("parallel", "arbitrary"),
        name="regroup_w_in",
    )(w_in)


def _mm_kernel(x_ref, w_ref, o_ref):
    o_ref[...] = jnp.dot(x_ref[...].astype(BF16), w_ref[...], preferred_element_type=F32).astype(o_ref.dtype)


def _matmul(x, w, *, tm, tn, out_dtype):
    m, k = x.shape
    n = w.shape[1]
    assert m % tm == 0 and n % tn == 0
    return pl.pallas_call(
        _mm_kernel,
        grid=(m // tm, n // tn),
        in_specs=[pl.BlockSpec((tm, k), lambda i, j: (i, 0)),
                  pl.BlockSpec((k, tn), lambda i, j: (0, j))],
        out_specs=pl.BlockSpec((tm, tn), lambda i, j: (i, j)),
        out_shape=jax.ShapeDtypeStruct((m, n), out_dtype),
        compiler_params=_cparams("parallel", "arbitrary"),
        name="proj_matmul",
    )(x, w)


def _store_residual_layernorm(branch, rows, res_ref, g_ref, b_ref, of_ref, ob_ref):
    y = DEEPNORM_ALPHA * res_ref[rows, :] + branch
    mu = jnp.mean(y, axis=-1, keepdims=True)
    yc = y - mu
    var = jnp.mean(yc * yc, axis=-1, keepdims=True)
    out = yc * lax.rsqrt(var + 1e-5) * g_ref[...] + b_ref[...]
    of_ref[rows, :] = out
    ob_ref[rows, :] = out.astype(BF16)


def _mm_ln_kernel(*refs, n_x, nk, sub):
    x_refs = refs[:n_x]
    w_ref, res_ref, g_ref, b_ref, of_ref, ob_ref, acc_ref = refs[n_x:]
    k = pl.program_id(1)

    def partial_product(rows):
        total, row = None, 0
        for x_ref in x_refs:
            width = x_ref.shape[1]
            part = jnp.dot(x_ref[rows, :], w_ref[row:row + width, :], preferred_element_type=F32)
            total = part if total is None else total + part
            row += width
        return total

    if nk > 1:
        @pl.when(k == 0)
        def _():
            acc_ref[...] = partial_product(slice(None))

        @pl.when(jnp.logical_and(k > 0, k < nk - 1))
        def _():
            acc_ref[...] += partial_product(slice(None))

    @pl.when(k == nk - 1)
    def _():
        for s in range(res_ref.shape[0] // sub):
            rows = slice(s * sub, (s + 1) * sub)
            acc = partial_product(rows)
            if nk > 1:
                acc = acc + acc_ref[rows, :]
            _store_residual_layernorm(acc, rows, res_ref, g_ref, b_ref, of_ref, ob_ref)


def _matmul_residual_ln(xs, w, res, g, b, *, tm, tk):
    m = res.shape[0]
    kdim, n = w.shape
    assert m % tm == 0 and kdim % tk == 0 and sum(x.shape[1] for x in xs) == kdim
    nk = kdim // tk
    assert len(xs) == 1 or nk == 1
    x_specs = [pl.BlockSpec((tm, tk if len(xs) == 1 else x.shape[1]), lambda i, k: (i, k)) for x in xs]
    return pl.pallas_call(
        functools.partial(_mm_ln_kernel, n_x=len(xs), nk=nk, sub=min(LN_SUB_ROWS, tm)),
        grid=(m // tm, nk),
        in_specs=x_specs + [pl.BlockSpec((tk, n), lambda i, k: (k, 0)),
                            pl.BlockSpec((tm, n), lambda i, k: (i, 0)),
                            pl.BlockSpec((1, n), lambda i, k: (0, 0)),
                            pl.BlockSpec((1, n), lambda i, k: (0, 0))],
        out_specs=[pl.BlockSpec((tm, n), lambda i, k: (i, 0)),
                   pl.BlockSpec((tm, n), lambda i, k: (i, 0))],
        out_shape=[jax.ShapeDtypeStruct((m, n), F32), jax.ShapeDtypeStruct((m, n), BF16)],
        scratch_shapes=[pltpu.VMEM((tm, n), F32)],
        compiler_params=_cparams("parallel", "arbitrary"),
        name="proj_residual_layernorm",
    )(*xs, w, res, g.reshape(1, n), b.reshape(1, n))


def _swiglu_kernel(x_ref, wg_ref, wu_ref, o_ref):
    x = x_ref[...]
    gate = jnp.dot(x, wg_ref[...], preferred_element_type=F32)
    up = jnp.dot(x, wu_ref[...], preferred_element_type=F32)
    o_ref[...] = (_silu(gate) * up).astype(o_ref.dtype)


def _swiglu_up(x, w_gate_up, *, tm, tn):
    m, k = x.shape
    h = w_gate_up.shape[1] // 2
    assert m % tm == 0 and h % tn == 0
    nj = h // tn
    return pl.pallas_call(
        _swiglu_kernel,
        grid=(m // tm, nj),
        in_specs=[pl.BlockSpec((tm, k), lambda i, j: (i, 0)),
                  pl.BlockSpec((k, tn), lambda i, j: (0, j)),
                  pl.BlockSpec((k, tn), lambda i, j: (0, j + nj))],
        out_specs=pl.BlockSpec((tm, tn), lambda i, j: (i, j)),
        out_shape=jax.ShapeDtypeStruct((m, h), BF16),
        compiler_params=_cparams("parallel", "arbitrary"),
        name="swiglu_up",
    )(x, w_gate_up, w_gate_up)


def _xattn_kernel(q_ref, k_ref, v_ref, w_ref, res_ref, g_ref, b_ref, of_ref, ob_ref, *, sub):
    scale = XATTN_DIM ** -0.5
    for s in range(q_ref.shape[0] // sub):
        rows = slice(s * sub, (s + 1) * sub)
        heads = range(XATTN_HEADS)
        cols = [slice(h * XATTN_DIM, (h + 1) * XATTN_DIM) for h in heads]
        logits = [lax.dot_general(q_ref[rows, cols[h]], k_ref[:, cols[h]], (((1,), (1,)), ((), ())),
                                  preferred_element_type=F32) * scale for h in heads]
        probs = []
        for h in heads:
            e = jnp.exp(logits[h] - jnp.max(logits[h], axis=-1, keepdims=True))
            probs.append((e / jnp.sum(e, axis=-1, keepdims=True)).astype(BF16))
        atts = [jnp.dot(probs[h], v_ref[:, cols[h]], preferred_element_type=F32).astype(BF16) for h in heads]
        branch = jnp.dot(atts[0], w_ref[cols[0], :], preferred_element_type=F32)
        for h in heads[1:]:
            branch = branch + jnp.dot(atts[h], w_ref[cols[h], :], preferred_element_type=F32)
        _store_residual_layernorm(branch, rows, res_ref, g_ref, b_ref, of_ref, ob_ref)


def _xattn_residual_ln(q, k, v, w, res, g, b, *, batch, tq):
    m, n = res.shape
    t = m // batch
    assert t % tq == 0
    nt = t // tq
    rows = lambda b, i: (b * nt + i, 0)
    const = lambda b, i: (0, 0)
    return pl.pallas_call(
        functools.partial(_xattn_kernel, sub=min(LN_SUB_ROWS, tq)),
        grid=(batch, nt),
        in_specs=[pl.BlockSpec((tq, D_MODEL), rows),
                  pl.BlockSpec((N_MEM, D_MODEL), lambda b, i: (b, 0)),
                  pl.BlockSpec((N_MEM, D_MODEL), lambda b, i: (b, 0)),
                  pl.BlockSpec((D_MODEL, n), const),
                  pl.BlockSpec((tq, n), rows),
                  pl.BlockSpec((1, n), const),
                  pl.BlockSpec((1, n), const)],
        out_specs=[pl.BlockSpec((tq, n), rows), pl.BlockSpec((tq, n), rows)],
        out_shape=[jax.ShapeDtypeStruct((m, n), F32), jax.ShapeDtypeStruct((m, n), BF16)],
        compiler_params=_cparams("parallel", "arbitrary"),
        name="xattn_residual_layernorm",
    )(q, k, v, w, res, g.reshape(1, n), b.reshape(1, n))


def _mixer_call(kernel_fn, proj, extra, extra_specs, out_width, scratch, *, batch, name):
    m, cols = proj.shape
    tb = min(MIXER_CHUNKS * CHUNK, m // batch)
    nt = m // batch // tb
    assert m == batch * nt * tb and tb % CHUNK == 0
    return pl.pallas_call(
        functools.partial(kernel_fn, nchunks=tb // CHUNK),
        grid=(batch, nt),
        in_specs=[pl.BlockSpec((tb, cols), lambda b, i: (b * nt + i, 0))] + extra_specs(tb, nt),
        out_specs=pl.BlockSpec((tb, out_width), lambda b, i: (b * nt + i, 0)),
        out_shape=jax.ShapeDtypeStruct((m, out_width), BF16),
        scratch_shapes=scratch(tb),
        compiler_params=_cparams("parallel", "arbitrary"),
        name=name,
    )(proj, *extra)


def _const_spec(a):
    return pl.BlockSpec(a.shape, lambda b, i: (0,) * a.ndim)


def _retention_kernel(p_ref, pos_ref, freq_ref, sign_ref, o_ref, state_ref, *, nchunks):
    c = CHUNK

    @pl.when(pl.program_id(1) == 0)
    def _():
        state_ref[...] = jnp.zeros_like(state_ref)

    heads = range(RET_HEADS)
    head_cols = lambda group, h: slice(group * RET_WIDTH + h * RET_DIM, group * RET_WIDTH + (h + 1) * RET_DIM)
    rows = lax.broadcasted_iota(jnp.int32, (c, c), 0)
    cols = lax.broadcasted_iota(jnp.int32, (c, c), 1)
    rel = (rows - cols).astype(F32)
    tpos = lax.broadcasted_iota(jnp.int32, (c, RET_DIM), 0).astype(F32)
    log_gammas = [math.log1p(-2.0 ** (-5.0 - h)) for h in heads]
    decay_masks = [jnp.where(rel >= 0, jnp.exp(jnp.where(rel >= 0, rel, 0.0) * lg), 0.0) for lg in log_gammas]
    q_decays = [jnp.exp((tpos + 1.0) * lg) for lg in log_gammas]
    k_decays = [jnp.exp((c - 1.0 - tpos) * lg) for lg in log_gammas]

    ang = pos_ref[...] * freq_ref[...]
    cos = jnp.cos(ang)
    sin_signed = jnp.sin(ang) * sign_ref[...]

    chunk_rows = [slice(ci * c, (ci + 1) * c) for ci in range(nchunks)]
    both = [(r, h) for r in chunk_rows for h in heads]

    def rotary(y, r):
        return y * cos[r] + pltpu.roll(y, RET_DIM // 2, 1) * sin_signed[r]

    qs = {(r.start, h): rotary(p_ref[r, head_cols(0, h)], r) for r, h in both}
    ks = {(r.start, h): rotary(p_ref[r, head_cols(1, h)], r) * RET_DIM ** -0.5 for r, h in both}
    vs = {(r.start, h): p_ref[r, head_cols(2, h)].astype(BF16) for r, h in both}
    scores = {(r.start, h): _bdot_nt(qs[r.start, h], ks[r.start, h]) * decay_masks[h] for r, h in both}
    intra = {(r.start, h): _bdot(scores[r.start, h], vs[r.start, h]) for r, h in both}
    kv = {(r.start, h): _bdot_tn(ks[r.start, h] * k_decays[h], vs[r.start, h]) for r, h in both}
    states = [state_ref[h] for h in heads]
    for r in chunk_rows:
        ys = [intra[r.start, h] + _bdot(qs[r.start, h] * q_decays[h], states[h]) for h in heads]
        states = [states[h] * math.exp(c * log_gammas[h]) + kv[r.start, h] for h in heads]
        for h in heads:
            y = ys[h]
            mu = jnp.mean(y, axis=-1, keepdims=True)
            yc = y - mu
            var = jnp.mean(yc * yc, axis=-1, keepdims=True)
            o_ref[r, head_cols(0, h)] = (_silu(p_ref[r, head_cols(3, h)])
                                         * (yc * lax.rsqrt(var + 1e-6))).astype(o_ref.dtype)
    for h in heads:
        state_ref[h] = states[h]


def _retention(proj, pos, *, batch):
    d = RET_DIM
    inv_freq = ROPE_BASE ** (-jnp.arange(0, d, 2, dtype=F32) / d)
    freq2 = jnp.concatenate([inv_freq, inv_freq]).reshape(1, d)
    sign = jnp.concatenate([-jnp.ones((d // 2,), F32), jnp.ones((d // 2,), F32)]).reshape(1, d)
    return _mixer_call(
        _retention_kernel, proj, (pos, freq2, sign),
        lambda tb, nt: [pl.BlockSpec((tb, 1), lambda b, i: (b * nt + i, 0)), _const_spec(freq2), _const_spec(sign)],
        RET_WIDTH, lambda tb: [pltpu.VMEM((RET_HEADS, d, d), F32)], batch=batch, name="retention")


def _gdn_kernel(p_ref, conv_ref, alog_ref, dtb_ref, nw_ref, o_ref, xpad_ref, qkv_ref, state_ref, *, nchunks):
    c = CHUNK
    tb = nchunks * c
    w3 = 3 * GDN_WIDTH

    @pl.when(pl.program_id(1) == 0)
    def _():
        state_ref[...] = jnp.zeros_like(state_ref)
        xpad_ref[0:HALO, :] = jnp.zeros((HALO, w3), F32)

    @pl.when(pl.program_id(1) != 0)
    def _():
        xpad_ref[0:HALO, :] = xpad_ref[tb:tb + HALO, :]

    xpad_ref[HALO:HALO + tb, :] = p_ref[:, 0:w3]
    xpad = xpad_ref[...]
    conv = conv_ref[CONV_K - 1:CONV_K, :] * xpad[HALO:HALO + tb]
    for back in range(1, CONV_K):
        shifted = pltpu.roll(xpad, back, 0)[HALO:HALO + tb]
        conv = conv + conv_ref[CONV_K - 1 - back:CONV_K - back, :] * shifted
    qkv_ref[...] = _silu(conv)

    causal, strict = _tri_masks(c)
    tril01 = jnp.where(causal, 1.0, 0.0).astype(BF16)
    eye = jnp.where(lax.broadcasted_iota(jnp.int32, (c, c), 0) == lax.broadcasted_iota(jnp.int32, (c, c), 1),
                    1.0, 0.0)
    right = lax.broadcasted_iota(jnp.int32, (c, 2 * c), 1) >= c
    heads = range(GDN_HEADS)
    head_cols = lambda group, h: slice(group * GDN_WIDTH + h * GDN_DIM, group * GDN_WIDTH + (h + 1) * GDN_DIM)

    def state_free(r):
        beta_all = _sigmoid(p_ref[r, 4 * GDN_WIDTH:4 * GDN_WIDTH + LANES])
        log_a = -jnp.exp(alog_ref[...]) * _softplus(p_ref[r, 4 * GDN_WIDTH + LANES:GDN_COLS] + dtb_ref[...])
        cum_all = _cumsum_rows(tril01, log_a)
        cum_rows = cum_all.T
        qs = [qkv_ref[r, head_cols(0, h)] for h in heads]
        ks = [qkv_ref[r, head_cols(1, h)] for h in heads]
        vs = [qkv_ref[r, head_cols(2, h)] for h in heads]
        qs = [q * lax.rsqrt(jnp.sum(q * q, axis=-1, keepdims=True) + 1e-6) * GDN_DIM ** -0.5 for q in qs]
        ks = [k * lax.rsqrt(jnp.sum(k * k, axis=-1, keepdims=True) + 1e-6) for k in ks]
        betas = [beta_all[:, h:h + 1] for h in heads]
        cums = [cum_all[:, h:h + 1] for h in heads]
        lasts = [cum_all[c - 1:c, h:h + 1] for h in heads]
        segs = [jnp.where(causal, jnp.exp(jnp.where(causal, cums[h] - cum_rows[h:h + 1, :], 0.0)), 0.0)
                for h in heads]
        kbs = [ks[h] * betas[h] for h in heads]
        ecums = [jnp.exp(cums[h]) for h in heads]
        scores = [_bdot_nt(jnp.concatenate([kbs[h], qs[h]], axis=0), ks[h]) for h in heads]
        return dict(
            r=r, qs=qs, ks=ks, ecums=ecums, cums=cums, lasts=lasts,
            inv_in=[jnp.concatenate([jnp.where(strict, -scores[h][0:c] * segs[h], 0.0), eye], axis=1)
                    for h in heads],
            qks=[scores[h][c:2 * c] * segs[h] for h in heads],
            rhs=[jnp.concatenate([vs[h] * betas[h], kbs[h] * ecums[h]], axis=1).astype(BF16) for h in heads])

    def chunk_group(j, carry):
        parts = [state_free(_chunk_rows(LOCKSTEP * j + ci)) for ci in range(LOCKSTEP)]
        t_invs = _unit_lower_inverses([w for part in parts for w in part["inv_in"]])
        states = [state_ref[h] for h in heads]
        for ci, part in enumerate(parts):
            uws = [_bdot(jnp.where(right, t_invs[ci * GDN_HEADS + h], 0.0),
                         jnp.concatenate([part["rhs"][h], part["rhs"][h]], axis=0)) for h in heads]
            wq_s = [_bdot(jnp.concatenate([uws[h][:, GDN_DIM:2 * GDN_DIM], part["qs"][h] * part["ecums"][h]], axis=0),
                          states[h]) for h in heads]
            v_news = [uws[h][:, 0:GDN_DIM] - wq_s[h][0:c] for h in heads]
            outs = [wq_s[h][c:2 * c] + _bdot(part["qks"][h], v_news[h]) for h in heads]
            states = [states[h] * jnp.exp(part["lasts"][h])
                      + _bdot_tn(part["ks"][h] * jnp.exp(part["lasts"][h] - part["cums"][h]), v_news[h])
                      for h in heads]
            for h in heads:
                o = outs[h]
                o = o * lax.rsqrt(jnp.mean(o * o, axis=-1, keepdims=True) + 1e-6) * nw_ref[...]
                o_ref[part["r"], head_cols(0, h)] = (o * _silu(p_ref[part["r"], head_cols(3, h)])).astype(o_ref.dtype)
        for h in heads:
            state_ref[h] = states[h]
        return carry

    assert nchunks % LOCKSTEP == 0
    lax.fori_loop(0, nchunks // LOCKSTEP, chunk_group, 0)


def _gated_deltanet(proj, conv_w, a_log, dt_bias, norm_w, *, batch):
    pad = lambda a: jnp.pad(a.reshape(1, -1), ((0, 0), (0, LANES - a.shape[-1])))
    extra = (conv_w, pad(a_log), pad(dt_bias), norm_w.reshape(1, GDN_DIM))
    return _mixer_call(
        _gdn_kernel, proj, extra, lambda tb, nt: [_const_spec(a) for a in extra], GDN_WIDTH,
        lambda tb: [pltpu.VMEM((tb + HALO, 3 * GDN_WIDTH), F32), pltpu.VMEM((tb, 3 * GDN_WIDTH), F32),
                    pltpu.VMEM((GDN_HEADS, GDN_DIM, GDN_DIM), F32)],
        batch=batch, name="gated_deltanet")


def _rwkv_kernel(p_ref, mu_ref, wup_ref, w0_ref, aup_ref, a0_ref, gup_ref, kk_ref, ka_ref, rk_ref,
                 lnw_ref, lnb_ref, o_ref, prev_ref, rkv_ref, logw_ref, asig_ref, gate_ref, state_ref, *, nchunks):
    c = CHUNK
    tb = nchunks * c
    wd_ = RWKV_WIDTH

    @pl.when(pl.program_id(1) == 0)
    def _():
        state_ref[...] = jnp.zeros_like(state_ref)
        prev_ref[0:HALO, :] = jnp.zeros((HALO, RWKV_IN), F32)

    @pl.when(pl.program_id(1) != 0)
    def _():
        prev_ref[0:HALO, :] = prev_ref[tb:tb + HALO, :]

    prev_ref[HALO:HALO + tb, :] = p_ref[...]
    p = p_ref[...]
    p = p + (pltpu.roll(prev_ref[...], 1, 0)[HALO:HALO + tb] - p) * mu_ref[...]
    rkv_ref[...] = p[:, 0:3 * wd_]
    w_down = p[:, 3 * wd_:3 * wd_ + W_LORA]
    a_down = p[:, 3 * wd_ + W_LORA:3 * wd_ + W_LORA + A_LORA]
    g_down = p[:, 3 * wd_ + W_LORA + A_LORA:RWKV_IN]
    logw_ref[...] = -math.exp(-0.5) * _sigmoid(w0_ref[...] + _bdot(jnp.tanh(w_down), wup_ref[...]))
    asig_ref[...] = _sigmoid(a0_ref[...] + _bdot(a_down, aup_ref[...]))
    gate_ref[...] = _bdot(_sigmoid(g_down), gup_ref[...])

    causal, strict = _tri_masks(c)
    tril01 = jnp.where(causal, 1.0, 0.0).astype(BF16)
    lane = lax.broadcasted_iota(jnp.int32, (c, LANES), 1)
    first = lane < RWKV_DIM
    rl = lax.broadcasted_iota(jnp.int32, (LANES, LANES), 0) < RWKV_DIM
    cl = lax.broadcasted_iota(jnp.int32, (LANES, LANES), 1) < RWKV_DIM
    same_head = rl == cl
    head_ones = jnp.where(same_head, 1.0, 0.0).astype(BF16)
    row2 = lax.broadcasted_iota(jnp.int32, (c, 2 * c), 0)
    col2 = lax.broadcasted_iota(jnp.int32, (c, 2 * c), 1)
    causal2 = row2 >= (col2 & (c - 1))
    right = col2 >= c
    strict_lo = row2 > col2
    strict_hi = jnp.logical_and(right, row2 + c > col2)
    eye_hi = jnp.where(col2 == row2 + c, 1.0, 0.0)
    pairs = range(RWKV_PAIRS)
    heads = range(RWKV_HEADS)
    pair_cols = lambda pr: slice(pr * LANES, (pr + 1) * LANES)
    own = [first if h % 2 == 0 else jnp.logical_not(first) for h in heads]
    zero = jnp.zeros((), BF16)

    def head_sum(y):
        return jnp.concatenate([_bdot(y[:, pair_cols(pr)], head_ones) for pr in pairs], axis=1)

    def state_free(r):
        r_all = rkv_ref[r, 0:wd_]
        k_in = rkv_ref[r, wd_:2 * wd_]
        v_all = rkv_ref[r, 2 * wd_:3 * wd_]
        log_w = logw_ref[r, :]
        a_sig = asig_ref[r, :]
        cum = _cumsum_rows(tril01, log_w)

        kk = k_in * kk_ref[...]
        kk = kk * lax.rsqrt(head_sum(kk * kk) + 1e-12)
        k_all = k_in * (1.0 + (a_sig - 1.0) * ka_ref[...])
        a_vec = -kk
        b_vec = kk * a_sig
        ref_c = cum[c // 2 - 1:c // 2, :]
        last = cum[c - 1:c, :]
        e_out = jnp.exp(ref_c - cum)
        e_tail = jnp.exp(last - cum)
        a_in = (a_vec * jnp.exp(cum - log_w - ref_c)).astype(BF16)
        r_in = (r_all * jnp.exp(cum - ref_c)).astype(BF16)
        a_prev = (a_vec * jnp.exp(cum - log_w)).astype(BF16)
        r_cum = (r_all * jnp.exp(cum)).astype(BF16)
        b_out = (b_vec * e_out).astype(BF16)
        k_out = (k_all * e_out).astype(BF16)
        k_tail = (k_all * e_tail).astype(BF16)
        b_tail = (b_vec * e_tail).astype(BF16)
        v_bf = v_all.astype(BF16)
        e_last = jnp.exp(last)

        outs_p = [jnp.concatenate([b_out[:, pair_cols(pr)], k_out[:, pair_cols(pr)]], axis=0) for pr in pairs]
        scores = []
        for h in heads:
            cols = pair_cols(h // 2)
            x = jnp.concatenate([jnp.where(own[h], a_in[:, cols], zero), jnp.where(own[h], r_in[:, cols], zero)],
                                axis=0)
            scores.append(lax.dot_general(x, outs_p[h // 2], (((1,), (1,)), ((), ())), preferred_element_type=F32))
        vv = [jnp.concatenate([v_bf[:, pair_cols(pr)], v_bf[:, pair_cols(pr)]], axis=0) for pr in pairs]
        ak_hi = [jnp.where(strict_hi, scores[h][0:c, :], 0.0).astype(BF16) for h in heads]
        return dict(
            r=r, v_bf=v_bf, v_all=v_all, e_last=e_last,
            inv_in=[jnp.where(strict_lo, scores[h][0:c, :], eye_hi) for h in heads],
            rbk=[jnp.where(causal2, scores[h][c:2 * c, :], 0.0).astype(BF16) for h in heads],
            akv=[jnp.dot(ak_hi[h], vv[h // 2], preferred_element_type=F32) for h in heads],
            reads=[jnp.concatenate([a_prev[:, pair_cols(pr)], r_cum[:, pair_cols(pr)]], axis=0) for pr in pairs],
            tails=[jnp.concatenate([k_tail[:, pair_cols(pr)], b_tail[:, pair_cols(pr)]], axis=0) for pr in pairs],
            bonus=head_sum(r_all * k_all * rk_ref[...]) * v_all)

    def chunk_group(j, carry):
        parts = [state_free(_chunk_rows(LOCKSTEP * j + ci)) for ci in range(LOCKSTEP)]
        t_invs = _unit_lower_inverses([w for part in parts for w in part["inv_in"]])
        states = [state_ref[pr] for pr in pairs]
        for ci, part in enumerate(parts):
            v_bf = part["v_bf"]
            read = [lax.dot_general(part["reads"][pr], states[pr].astype(BF16), (((1,), (1,)), ((), ())),
                                    preferred_element_type=F32) for pr in pairs]
            rhs = [read[pr][0:c] + jnp.where(first, part["akv"][2 * pr], part["akv"][2 * pr + 1]) for pr in pairs]
            rhs2 = [jnp.concatenate([rhs[pr].astype(BF16)] * 2, axis=0) for pr in pairs]
            u_h = [_bdot(jnp.where(right, t_invs[ci * RWKV_HEADS + h], 0.0), rhs2[h // 2]) for h in heads]
            u_p = [jnp.where(first, u_h[2 * pr], u_h[2 * pr + 1]) for pr in pairs]
            uv = [jnp.concatenate([u_p[pr].astype(BF16), v_bf[:, pair_cols(pr)]], axis=0) for pr in pairs]
            y_h = [jnp.dot(part["rbk"][h], uv[h // 2], preferred_element_type=F32) for h in heads]
            y_all = jnp.concatenate([read[pr][c:2 * c] + jnp.where(first, y_h[2 * pr], y_h[2 * pr + 1])
                                     for pr in pairs], axis=1)
            new_states = []
            for pr in pairs:
                cols = pair_cols(pr)
                upd = lax.dot_general(jnp.concatenate([v_bf[:, cols], u_p[pr].astype(BF16)], axis=0),
                                      part["tails"][pr], (((0,), (0,)), ((), ())), preferred_element_type=F32)
                new_states.append(states[pr] * part["e_last"][:, cols] + jnp.where(same_head, upd, 0.0))
            states = new_states
            part["y"] = y_all
        for pr in pairs:
            state_ref[pr] = states[pr]

        means = [head_sum(part["y"]) * (1.0 / RWKV_DIM) for part in parts]
        centred = [part["y"] - mean for part, mean in zip(parts, means)]
        variances = [head_sum(yc * yc) * (1.0 / RWKV_DIM) for yc in centred]
        for part, yc, var in zip(parts, centred, variances):
            yn = yc * lax.rsqrt(var + RWKV_LNX_EPS) * lnw_ref[...] + lnb_ref[...] + part["bonus"]
            o_ref[part["r"], :] = (yn * gate_ref[part["r"], :]).astype(o_ref.dtype)
        return carry

    assert nchunks % LOCKSTEP == 0
    lax.fori_loop(0, nchunks // LOCKSTEP, chunk_group, 0)


def _rwkv7(proj, mu, w_up, w0, a_up, a0, g_up, k_k, k_a, r_k, lnx_w, lnx_b, *, batch):
    row = lambda a: a.reshape(1, -1)
    params = (row(mu), w_up.astype(BF16), row(w0), a_up.astype(BF16), row(a0), g_up.astype(BF16),
              row(k_k), row(k_a), row(r_k), row(lnx_w), row(lnx_b))
    return _mixer_call(
        _rwkv_kernel, proj, params, lambda tb, nt: [_const_spec(a) for a in params], RWKV_WIDTH,
        lambda tb: [pltpu.VMEM((tb + HALO, RWKV_IN), F32), pltpu.VMEM((tb, 3 * RWKV_WIDTH), F32),
                    pltpu.VMEM((tb, RWKV_WIDTH), F32), pltpu.VMEM((tb, RWKV_WIDTH), F32),
                    pltpu.VMEM((tb, RWKV_WIDTH), F32), pltpu.VMEM((RWKV_PAIRS, LANES, LANES), F32)],
        batch=batch, name="rwkv7")


def kernel(x, mem, positions, w_in, gdn_conv, gdn_a_log, gdn_dt_bias, gdn_norm, rwkv_mu, rwkv_w_up, rwkv_w0, rwkv_a_up, rwkv_a0, rwkv_g_up, rwkv_k_k, rwkv_k_a, rwkv_r_k, rwkv_lnx_w, rwkv_lnx_b, w_out, ln1_g, ln1_b, xattn_q, xattn_k, xattn_v, xattn_o, ln2_g, ln2_b, ffn_gate_up, ffn_down, ln3_g, ln3_b):
    batch, seq, d = x.shape
    m = batch * seq
    tm = min(1024, m)
    tm_ln = min(512, m)
    xf = x.reshape(m, d)
    xb = xf.astype(BF16)
    memb = mem.reshape(batch * N_MEM, d).astype(BF16)
    pos = positions.astype(F32).reshape(m, 1)
    w_gdn, w_ret, w_rwkv = _regroup_w_in(w_in, tk=256)

    for l in range(DEPTH):
        p_gdn = _matmul(xb, w_gdn[l], tm=tm, tn=GDN_COLS // 2, out_dtype=F32)
        p_ret = _matmul(xb, w_ret[l], tm=tm, tn=4 * RET_WIDTH // 2, out_dtype=F32)
        p_rwkv = _matmul(xb, w_rwkv[l], tm=tm, tn=RWKV_IN // 2, out_dtype=F32)
        y_a = _gated_deltanet(p_gdn, gdn_conv[l], gdn_a_log[l], gdn_dt_bias[l], gdn_norm[l], batch=batch)
        y_b = _retention(p_ret, pos, batch=batch)
        y_c = _rwkv7(p_rwkv, rwkv_mu[l], rwkv_w_up[l], rwkv_w0[l], rwkv_a_up[l], rwkv_a0[l], rwkv_g_up[l],
                     rwkv_k_k[l], rwkv_k_a[l], rwkv_r_k[l], rwkv_lnx_w[l], rwkv_lnx_b[l], batch=batch)
        xf, xb = _matmul_residual_ln([y_a, y_b, y_c], w_out[l].astype(BF16), xf, ln1_g[l], ln1_b[l],
                                     tm=tm_ln, tk=d)

        q = _matmul(xb, xattn_q[l].astype(BF16), tm=tm, tn=d // 2, out_dtype=BF16)
        kmem = _matmul(memb, xattn_k[l].astype(BF16), tm=tm, tn=d // 2, out_dtype=BF16)
        vmem = _matmul(memb, xattn_v[l].astype(BF16), tm=tm, tn=d // 2, out_dtype=BF16)
        xf, xb = _xattn_residual_ln(q, kmem, vmem, xattn_o[l].astype(BF16), xf, ln2_g[l], ln2_b[l],
                                    batch=batch, tq=min(tm_ln, seq))

        hidden = _swiglu_up(xb, ffn_gate_up[l].astype(BF16), tm=tm, tn=512)
        xf, xb = _matmul_residual_ln([hidden], ffn_down[l].astype(BF16), xf, ln3_g[l], ln3_b[l],
                                     tm=tm_ln, tk=ffn_down.shape[1] // FFN_DOWN_K_STEPS)

    return xf.reshape(batch, seq, d)
```
